```python
import jax, jax.numpy as jnp
from jax import lax
import numpy as np

D_MODEL = 1024
BATCH = 4
SEQ = 8192
DEPTH = 1

HEAD_DIM = 64
N_HEADS_SB = 8
N_HEADS_FOX = 8
D_SB = N_HEADS_SB * HEAD_DIM
D_FOX = N_HEADS_FOX * HEAD_DIM
D_MIX = D_SB + D_FOX
D_IN = 3 * D_SB + 3 * D_FOX + N_HEADS_FOX
D_FF = 2816
D_PLE = 256
BLOCK_Q = 128
EPS = 1e-6
FFN_RES_WEIGHT = 0.5
FORGET_BIAS_INIT = 3.0

kernel_name = "hybrid_stickbreak_fox_macaron_block"


def rms_norm(x, g):
    xf = x.astype(jnp.float32)
    y = xf * lax.rsqrt(jnp.mean(xf * xf, axis=-1, keepdims=True) + EPS)
    return (y * g.astype(jnp.float32)).astype(x.dtype)


def swiglu(h, w_gate, w_up, w_down):
    return (jax.nn.silu(h @ w_gate) * (h @ w_up)) @ w_down


def split_heads(t, n_heads):
    b, s, _ = t.shape
    return t.reshape(b, s, n_heads, HEAD_DIM).transpose(0, 2, 1, 3)


def merge_heads(t):
    b, h, s, d = t.shape
    return t.transpose(0, 2, 1, 3).reshape(b, s, h * d)


def stick_breaking_block(q_blk, k_pre, v_pre, q_start):
    n_q = q_blk.shape[2]
    n_k = k_pre.shape[2]
    z = jnp.einsum('bhqd,bhkd->bhqk', q_blk.astype(jnp.float32), k_pre.astype(jnp.float32)) * (HEAD_DIM ** -0.5)
    t_idx = q_start + jnp.arange(n_q)[:, None]
    s_idx = jnp.arange(n_k)[None, :]
    causal = s_idx < t_idx
    log_1m_beta = jnp.where(causal, jax.nn.log_sigmoid(-z), 0.0)
    tail = lax.cumsum(log_1m_beta, axis=3, reverse=True) - log_1m_beta
    a = jnp.where(causal, jnp.exp(jax.nn.log_sigmoid(z) + tail), 0.0)
    return jnp.einsum('bhqk,bhkd->bhqd', a, v_pre.astype(jnp.float32))


def forgetting_block(q_blk, k_pre, v_pre, fcum_q, fcum_k, q_start):
    n_q = q_blk.shape[2]
    n_k = k_pre.shape[2]
    logits = jnp.einsum('bhqd,bhkd->bhqk', q_blk.astype(jnp.float32), k_pre.astype(jnp.float32)) * (HEAD_DIM ** -0.5)
    logits = logits + fcum_q[..., :, None] - fcum_k[..., None, :]
    t_idx = q_start + jnp.arange(n_q)[:, None]
    s_idx = jnp.arange(n_k)[None, :]
    logits = jnp.where(s_idx <= t_idx, logits, -jnp.inf)
    w = jax.nn.softmax(logits, axis=-1)
    return jnp.einsum('bhqk,bhkd->bhqd', w, v_pre.astype(jnp.float32))


def token_mixer(h, w_in, b_forget, sb_group_g, fox_group_g, w_out):
    b, s, _ = h.shape
    proj = h @ w_in
    o = np.cumsum([0, D_SB, D_SB, D_SB, D_FOX, D_FOX, D_FOX, N_HEADS_FOX])
    q_sb = split_heads(proj[..., o[0]:o[1]], N_HEADS_SB)
    k_sb = split_heads(proj[..., o[1]:o[2]], N_HEADS_SB)
    v_sb = split_heads(proj[..., o[2]:o[3]], N_HEADS_SB)
    q_fx = split_heads(proj[..., o[3]:o[4]], N_HEADS_FOX)
    k_fx = split_heads(proj[..., o[4]:o[5]], N_HEADS_FOX)
    v_fx = split_heads(proj[..., o[5]:o[6]], N_HEADS_FOX)
    f_logit = proj[..., o[6]:o[7]].astype(jnp.float32) + b_forget.astype(jnp.float32)
    fcum = lax.cumsum(jax.nn.log_sigmoid(f_logit), axis=1).transpose(0, 2, 1)

    outs_sb, outs_fx = [], []
    for i in range(s // BLOCK_Q):
        s0 = i * BLOCK_Q
        s1 = s0 + BLOCK_Q
        outs_sb.append(stick_breaking_block(q_sb[:, :, s0:s1], k_sb[:, :, :s1], v_sb[:, :, :s1], s0))
        outs_fx.append(forgetting_block(q_fx[:, :, s0:s1], k_fx[:, :, :s1], v_fx[:, :, :s1],
                                        fcum[:, :, s0:s1], fcum[:, :, :s1], s0))
    o_sb = merge_heads(jnp.concatenate(outs_sb, axis=2))
    o_fx = merge_heads(jnp.concatenate(outs_fx, axis=2))
    merged = jnp.concatenate([rms_norm(o_sb, sb_group_g), rms_norm(o_fx, fox_group_g)], axis=-1)
    return merged.astype(h.dtype) @ w_out


def setup_inputs(seed: int = 0) -> dict:
    key = jax.random.key(seed)
    ks = iter(jax.random.split(key, 32))

    def dense(shape, fan_in):
        return jax.random.normal(next(ks), (DEPTH,) + shape, jnp.float32) * (fan_in ** -0.5)

    def gain(dim):
        return 1.0 + 0.02 * jax.random.normal(next(ks), (DEPTH, dim), jnp.float32)

    inp = {}
    inp["x"] = jax.random.normal(next(ks), (BATCH, SEQ, D_MODEL), jnp.float32)
    inp["p"] = jax.random.normal(next(ks), (DEPTH, BATCH, SEQ, D_PLE), jnp.float32)
    inp["ffn1_pre_g"] = gain(D_MODEL)
    inp["ffn1_w_gate"] = dense((D_MODEL, D_FF), D_MODEL)
    inp["ffn1_w_up"] = dense((D_MODEL, D_FF), D_MODEL)
    inp["ffn1_w_down"] = dense((D_FF, D_MODEL), D_FF)
    inp["ffn1_post_g"] = gain(D_MODEL)
    inp["mix_pre_g"] = gain(D_MODEL)
    inp["w_in"] = dense((D_MODEL, D_IN), D_MODEL)
    inp["b_forget"] = FORGET_BIAS_INIT + 0.5 * jax.random.normal(next(ks), (DEPTH, N_HEADS_FOX), jnp.float32)
    inp["sb_group_g"] = gain(D_SB)
    inp["fox_group_g"] = gain(D_FOX)
    inp["w_out"] = dense((D_MIX, D_MODEL), D_MIX)
    inp["mix_post_g"] = gain(D_MODEL)
    inp["ffn2_pre_g"] = gain(D_MODEL)
    inp["ffn2_w_gate"] = dense((D_MODEL, D_FF), D_MODEL)
    inp["ffn2_w_up"] = dense((D_MODEL, D_FF), D_MODEL)
    inp["ffn2_w_down"] = dense((D_FF, D_MODEL), D_FF)
    inp["ffn2_post_g"] = gain(D_MODEL)
    inp["ple_pre_g"] = gain(D_MODEL)
    inp["w_ple_gate"] = dense((D_MODEL, D_MODEL), D_MODEL)
    inp["w_ple_proj"] = dense((D_PLE, D_MODEL), D_PLE)
    inp["ple_post_g"] = gain(D_MODEL)
    return inp


def reference(x, p, ffn1_pre_g, ffn1_w_gate, ffn1_w_up, ffn1_w_down, ffn1_post_g,
              mix_pre_g, w_in, b_forget, sb_group_g, fox_group_g, w_out, mix_post_g,
              ffn2_pre_g, ffn2_w_gate, ffn2_w_up, ffn2_w_down, ffn2_post_g,
              ple_pre_g, w_ple_gate, w_ple_proj, ple_post_g):
    for i in range(DEPTH):
        f1 = swiglu(rms_norm(x, ffn1_pre_g[i]), ffn1_w_gate[i], ffn1_w_up[i], ffn1_w_down[i])
        x = x + FFN_RES_WEIGHT * rms_norm(f1, ffn1_post_g[i])
        m = token_mixer(rms_norm(x, mix_pre_g[i]), w_in[i], b_forget[i],
                        sb_group_g[i], fox_group_g[i], w_out[i])
        x = x + rms_norm(m, mix_post_g[i])
        f2 = swiglu(rms_norm(x, ffn2_pre_g[i]), ffn2_w_gate[i], ffn2_w_up[i], ffn2_w_down[i])
        x = x + FFN_RES_WEIGHT * rms_norm(f2, ffn2_post_g[i])
        gate = jax.nn.sigmoid(rms_norm(x, ple_pre_g[i]) @ w_ple_gate[i])
        e = gate * (p[i] @ w_ple_proj[i])
        x = x + rms_norm(e, ple_post_g[i])
    return x
```

```python
import functools

import jax
import jax.numpy as jnp
from jax import lax
from jax.experimental import pallas as pl
from jax.experimental.pallas import tpu as pltpu

HEAD_DIM = 64
N_HEADS_SB = 8
N_HEADS_FOX = 8
EPS = 1e-6
FFN_RES_WEIGHT = 0.5

LANES = 128
MXU_DTYPE = jnp.bfloat16
VMEM_LIMIT_BYTES = 48 * 1024 * 1024

FFN_ROWS = 512
FFN_COLS = 256
PROJ_ROWS = 512
ATTN_BLOCK = 256

F32 = jnp.float32
NEG_BIG = -1e30


def _dot(a, b):
    return jnp.dot(a, b, preferred_element_type=F32)


def _rms(x, g):
    return x * lax.rsqrt(jnp.mean(x * x, axis=-1, keepdims=True) + EPS) * g


def _split3(x):
    hi = x.astype(MXU_DTYPE).astype(F32)
    r = x - hi
    mid = r.astype(MXU_DTYPE).astype(F32)
    lo = (r - mid).astype(MXU_DTYPE).astype(F32)
    return hi, mid, lo


def _params(n_axes):
    return pltpu.CompilerParams(
        dimension_semantics=("arbitrary",) * n_axes,
        vmem_limit_bytes=VMEM_LIMIT_BYTES,
    )


def _ffn_kernel(x_ref, pre_g_ref, wg_ref, wu_ref, wd_ref, post_g_ref, o_ref, hn_ref, acc_ref):
    c = pl.program_id(1)

    @pl.when(c == 0)
    def _():
        hn_ref[...] = _rms(x_ref[...], pre_g_ref[...]).astype(MXU_DTYPE)
        acc_ref[...] = jnp.zeros_like(acc_ref)

    hn = hn_ref[...]
    g = _dot(hn, wg_ref[...])
    u = _dot(hn, wu_ref[...])
    act = (g * jax.nn.sigmoid(g)) * u
    acc_ref[...] += _dot(act.astype(MXU_DTYPE), wd_ref[...])

    @pl.when(c == pl.num_programs(1) - 1)
    def _():
        o_ref[...] = x_ref[...] + FFN_RES_WEIGHT * _rms(acc_ref[...], post_g_ref[...])


def _ffn(x2, pre_g, wg, wu, wd, post_g):
    n, d = x2.shape
    dff = wg.shape[1]
    tm = min(FFN_ROWS, n)
    tf = FFN_COLS
    return pl.pallas_call(
        _ffn_kernel,
        grid=(n // tm, dff // tf),
        in_specs=[
            pl.BlockSpec((tm, d), lambda i, c: (i, 0)),
            pl.BlockSpec((1, d), lambda i, c: (0, 0)),
            pl.BlockSpec((d, tf), lambda i, c: (0, c)),
            pl.BlockSpec((d, tf), lambda i, c: (0, c)),
            pl.BlockSpec((tf, d), lambda i, c: (c, 0)),
            pl.BlockSpec((1, d), lambda i, c: (0, 0)),
        ],
        out_specs=pl.BlockSpec((tm, d), lambda i, c: (i, 0)),
        out_shape=jax.ShapeDtypeStruct((n, d), F32),
        scratch_shapes=[pltpu.VMEM((tm, d), MXU_DTYPE), pltpu.VMEM((tm, d), F32)],
        compiler_params=_params(2),
        name="ffn",
    )(x2, pre_g, wg, wu, wd, post_g)


def _log_sigmoid(x):
    return jnp.minimum(x, 0.0) - jnp.log(1.0 + jnp.exp(-jnp.abs(x)))


def _inproj_kernel(x_ref, g_ref, wqkv_ref, wf_ref, bf_ref, q_ref, kt_ref, v_ref, carry_ref, *, tk):
    tm = x_ref.shape[1]
    d_sb = N_HEADS_SB * HEAD_DIM
    d_fx = N_HEADS_FOX * HEAD_DIM
    si = pl.program_id(1)

    @pl.when(si == 0)
    def _():
        carry_ref[...] = jnp.zeros_like(carry_ref)

    hn = _rms(x_ref[0], g_ref[...]).astype(MXU_DTYPE)

    ls = _log_sigmoid(_dot(hn, wf_ref[...]) + bf_ref[...])
    r = lax.broadcasted_iota(jnp.int32, (tm, tm), 0)
    c = lax.broadcasted_iota(jnp.int32, (tm, tm), 1)
    tri = (r >= c).astype(MXU_DTYPE)
    ls_hi, ls_mid, ls_lo = _split3(ls)
    fcum = (_dot(tri, ls_hi.astype(MXU_DTYPE)) + _dot(tri, ls_mid.astype(MXU_DTYPE))
            + _dot(tri, ls_lo.astype(MXU_DTYPE))) + carry_ref[...]
    carry_ref[...] = fcum[tm - 1:tm, :]

    lane = lax.broadcasted_iota(jnp.int32, (tm, LANES), 1)
    low = lane < HEAD_DIM

    def head_halves(col0):
        pr = _dot(hn, wqkv_ref[:, col0:col0 + LANES])
        return pr, pltpu.roll(pr, HEAD_DIM, 1)

    def store_k(h, k_aug):
        kt = k_aug.T
        for j in range(tm // tk):
            kt_ref[0, h, j] = kt[:, j * tk:(j + 1) * tk].astype(MXU_DTYPE)

    scale = HEAD_DIM ** -0.5
    for grp in range(d_sb // LANES):
        for half, pr in enumerate(head_halves(grp * LANES)):
            q_ref[0, 2 * grp + half] = jnp.where(low, pr * scale, 0.0).astype(MXU_DTYPE)
        for half, pr in enumerate(head_halves(d_sb + grp * LANES)):
            store_k(2 * grp + half, jnp.where(low, pr, 0.0))

    q0 = 3 * d_sb
    for grp in range(d_fx // LANES):
        qs = head_halves(q0 + grp * LANES)
        ks = head_halves(q0 + d_fx + grp * LANES)
        for half in range(2):
            hf = 2 * grp + half
            f_hi, f_mid, f_lo = _split3(fcum[:, hf:hf + 1])
            q_bias = jnp.where(lane == HEAD_DIM, f_hi,
                     jnp.where(lane == HEAD_DIM + 1, f_mid,
                     jnp.where(lane == HEAD_DIM + 2, f_lo,
                     jnp.where(lane < HEAD_DIM + 6, 1.0, 0.0))))
            k_bias = jnp.where(lane < HEAD_DIM + 3, 1.0,
                     jnp.where(lane == HEAD_DIM + 3, -f_hi,
                     jnp.where(lane == HEAD_DIM + 4, -f_mid,
                     jnp.where(lane == HEAD_DIM + 5, -f_lo, 0.0))))
            q_ref[0, N_HEADS_SB + hf] = jnp.where(low, qs[half] * scale, q_bias).astype(MXU_DTYPE)
            store_k(N_HEADS_SB + hf, jnp.where(low, ks[half], k_bias))

    v_ref[0, :, 0:d_sb] = _dot(hn, wqkv_ref[:, 2 * d_sb:3 * d_sb]).astype(MXU_DTYPE)
    v_ref[0, :, d_sb:d_sb + d_fx] = _dot(
        hn, wqkv_ref[:, q0 + 2 * d_fx:q0 + 3 * d_fx]).astype(MXU_DTYPE)


def _in_proj(x3, g, wqkv, wf, bf):
    b, s, d = x3.shape
    tm = min(PROJ_ROWS, s)
    tk = min(ATTN_BLOCK, s)
    nh = N_HEADS_SB + N_HEADS_FOX
    dv = (N_HEADS_SB + N_HEADS_FOX) * HEAD_DIM
    return pl.pallas_call(
        functools.partial(_inproj_kernel, tk=tk),
        grid=(b, s // tm),
        in_specs=[
            pl.BlockSpec((1, tm, d), lambda i, j: (i, j, 0)),
            pl.BlockSpec((1, d), lambda i, j: (0, 0)),
            pl.BlockSpec(wqkv.shape, lambda i, j: (0, 0)),
            pl.BlockSpec(wf.shape, lambda i, j: (0, 0)),
            pl.BlockSpec((1, LANES), lambda i, j: (0, 0)),
        ],
        out_specs=[
            pl.BlockSpec((1, nh, tm, LANES), lambda i, j: (i, 0, j, 0)),
            pl.BlockSpec((1, nh, tm // tk, LANES, tk), lambda i, j: (i, 0, j, 0, 0)),
            pl.BlockSpec((1, tm, dv), lambda i, j: (i, j, 0)),
        ],
        out_shape=[
            jax.ShapeDtypeStruct((b, nh, s, LANES), MXU_DTYPE),
            jax.ShapeDtypeStruct((b, nh, s // tk, LANES, tk), MXU_DTYPE),
            jax.ShapeDtypeStruct((b, s, dv), MXU_DTYPE),
        ],
        scratch_shapes=[pltpu.VMEM((1, LANES), F32)],
        compiler_params=_params(2),
        name="in_proj",
    )(x3, g, wqkv, wf, bf)


def _sb_kernel(q_ref, kt_ref, v_ref, o_ref):
    t = q_ref.shape[2]
    qi = pl.program_id(2)
    row = lax.broadcasted_iota(jnp.int32, (t, t), 0)
    col = lax.broadcasted_iota(jnp.int32, (t, t), 1)
    strict = col < row
    jj = lax.broadcasted_iota(jnp.int32, (2 * t, t), 0)
    ss = lax.broadcasted_iota(jnp.int32, (2 * t, t), 1)
    suffix = ((jj >= ss) & ((jj < t) | (jj - t >= ss))).astype(MXU_DTYPE)

    def chunk(h, kc, carry_r, acc, diagonal):
        q = q_ref[0, h]
        z = _dot(q, kt_ref[0, h, kc])
        sp = jnp.maximum(z, 0.0) + jnp.log(1.0 + jnp.exp(-jnp.abs(z)))
        if diagonal:
            sp = jnp.where(strict, sp, 0.0)
        hi = sp.astype(MXU_DTYPE)
        lo = (sp - hi.astype(F32)).astype(MXU_DTYPE)
        csum = _dot(jnp.concatenate([hi, lo], axis=1), suffix)
        a = jnp.exp(z + carry_r - csum)
        if diagonal:
            a = jnp.where(strict, a, 0.0)
        v = v_ref[0, pl.ds(pl.multiple_of(kc * t, t), t), :]
        acc = acc + _dot(a.astype(MXU_DTYPE), v)
        return carry_r - csum[:, 0:1], acc

    accs = []
    for h in range(2):
        carry = chunk(h, qi, jnp.zeros((t, 1), F32), jnp.zeros((t, LANES), F32), True)
        carry = lax.fori_loop(
            0, qi, lambda i, cr, h=h: chunk(h, qi - 1 - i, cr[0], cr[1], False), carry)
        accs.append(carry[1])
    lane = lax.broadcasted_iota(jnp.int32, (t, LANES), 1)
    o_ref[0] = jnp.where(lane < HEAD_DIM, accs[0], accs[1])


def _fox_kernel(q_ref, kt_ref, v_ref, o_ref):
    t = q_ref.shape[2]
    qi = pl.program_id(2)
    row = lax.broadcasted_iota(jnp.int32, (t, t), 0)
    col = lax.broadcasted_iota(jnp.int32, (t, t), 1)
    causal = col <= row

    def chunk(h, kc, m, l, acc, diagonal):
        q = q_ref[0, h]
        s = _dot(q, kt_ref[0, h, kc])
        if diagonal:
            s = jnp.where(causal, s, NEG_BIG)
        m_new = jnp.maximum(m, jnp.max(s, axis=1, keepdims=True))
        alpha = jnp.exp(m - m_new)
        p = jnp.exp(s - m_new)
        l = alpha * l + jnp.sum(p, axis=1, keepdims=True)
        v = v_ref[0, pl.ds(pl.multiple_of(kc * t, t), t), :]
        acc = alpha * acc + _dot(p.astype(MXU_DTYPE), v)
        return m_new, l, acc

    outs = []
    for h in range(2):
        carry = chunk(h, qi, jnp.full((t, 1), NEG_BIG, F32), jnp.zeros((t, 1), F32),
                      jnp.zeros((t, LANES), F32), True)
        carry = lax.fori_loop(
            0, qi, lambda i, cr, h=h: chunk(h, qi - 1 - i, cr[0], cr[1], cr[2], False), carry)
        outs.append(carry[2] / carry[1])
    lane = lax.broadcasted_iota(jnp.int32, (t, LANES), 1)
    o_ref[0] = jnp.where(lane < HEAD_DIM, outs[0], outs[1])


def _attention(body, name, q_aug, kt_aug, v, head0, n_heads):
    b, _, s, _ = q_aug.shape
    nk, t = kt_aug.shape[2], kt_aug.shape[4]
    hp0 = head0 // 2
    return pl.pallas_call(
        body,
        grid=(b, n_heads // 2, s // t),
        in_specs=[
            pl.BlockSpec((1, 2, t, LANES), lambda i, p, j: (i, hp0 + p, j, 0)),
            pl.BlockSpec((1, 2, nk, LANES, t), lambda i, p, j: (i, hp0 + p, 0, 0, 0)),
            pl.BlockSpec((1, s, LANES), lambda i, p, j: (i, 0, hp0 + p)),
        ],
        out_specs=pl.BlockSpec((1, t, LANES), lambda i, p, j: (i, j, p)),
        out_shape=jax.ShapeDtypeStruct((b, s, n_heads * HEAD_DIM), F32),
        compiler_params=_params(3),
        name=name,
    )(q_aug, kt_aug, v)


def _outproj_kernel(osb_ref, ofx_ref, x_ref, gsb_ref, gfx_ref, wout_ref, gpost_ref, o_ref):
    d_sb = osb_ref.shape[1]
    nsb = _rms(osb_ref[...], gsb_ref[...]).astype(MXU_DTYPE)
    nfx = _rms(ofx_ref[...], gfx_ref[...]).astype(MXU_DTYPE)
    m = _dot(nsb, wout_ref[0:d_sb, :]) + _dot(nfx, wout_ref[d_sb:, :])
    o_ref[...] = x_ref[...] + _rms(m, gpost_ref[...])


def _out_proj(o_sb, o_fx, x2, g_sb, g_fx, w_out, g_post):
    n, d = x2.shape
    tm = min(PROJ_ROWS, n)
    row = lambda w: pl.BlockSpec((tm, w), lambda i: (i, 0))
    full = lambda a: pl.BlockSpec(a.shape, lambda i: (0, 0))
    return pl.pallas_call(
        _outproj_kernel,
        grid=(n // tm,),
        in_specs=[row(o_sb.shape[1]), row(o_fx.shape[1]), row(d), full(g_sb), full(g_fx),
                  full(w_out), full(g_post)],
        out_specs=row(d),
        out_shape=jax.ShapeDtypeStruct((n, d), F32),
        compiler_params=_params(1),
        name="out_proj",
    )(o_sb, o_fx, x2, g_sb, g_fx, w_out, g_post)


def _ple_kernel(x_ref, p_ref, gpre_ref, wgate_ref, wproj_ref, gpost_ref, o_ref):
    x = x_ref[...]
    hn = _rms(x, gpre_ref[...]).astype(MXU_DTYPE)
    gate = jax.nn.sigmoid(_dot(hn, wgate_ref[...]))
    e = gate * _dot(p_ref[...].astype(MXU_DTYPE), wproj_ref[...])
    o_ref[...] = x + _rms(e, gpost_ref[...])


def _ple(x2, p2, g_pre, w_gate, w_proj, g_post):
    n, d = x2.shape
    tm = min(PROJ_ROWS, n)
    row = lambda w: pl.BlockSpec((tm, w), lambda i: (i, 0))
    full = lambda a: pl.BlockSpec(a.shape, lambda i: (0, 0))
    return pl.pallas_call(
        _ple_kernel,
        grid=(n // tm,),
        in_specs=[row(d), row(p2.shape[1]), full(g_pre), full(w_gate), full(w_proj), full(g_post)],
        out_specs=row(d),
        out_shape=jax.ShapeDtypeStruct((n, d), F32),
        compiler_params=_params(1),
        name="ple",
    )(x2, p2, g_pre, w_gate, w_proj, g_post)


def kernel(x, p, ffn1_pre_g, ffn1_w_gate, ffn1_w_up, ffn1_w_down, ffn1_post_g, mix_pre_g, w_in, b_forget, sb_group_g, fox_group_g, w_out, mix_post_g, ffn2_pre_g, ffn2_w_gate, ffn2_w_up, ffn2_w_down, ffn2_post_g, ple_pre_g, w_ple_gate, w_ple_proj, ple_post_g):
    b, s, d = x.shape
    depth = p.shape[0]
    n = b * s
    d_qkv = 3 * (N_HEADS_SB + N_HEADS_FOX) * HEAD_DIM
    w = lambda a: a.astype(MXU_DTYPE)
    x2 = x.reshape(n, d)
    for i in range(depth):
        x2 = _ffn(x2, ffn1_pre_g[i:i + 1], w(ffn1_w_gate[i]), w(ffn1_w_up[i]), w(ffn1_w_down[i]),
                  ffn1_post_g[i:i + 1])
        wf = jnp.pad(w_in[i][:, d_qkv:], ((0, 0), (0, LANES - N_HEADS_FOX)))
        bf = jnp.pad(b_forget[i:i + 1], ((0, 0), (0, LANES - N_HEADS_FOX)))
        q_aug, kt_aug, v = _in_proj(x2.reshape(b, s, d), mix_pre_g[i:i + 1], w(w_in[i][:, :d_qkv]),
                                    w(wf), bf)
        o_sb = _attention(_sb_kernel, "attn_sb", q_aug, kt_aug, v, 0, N_HEADS_SB)
        o_fx = _attention(_fox_kernel, "attn_fox", q_aug, kt_aug, v, N_HEADS_SB, N_HEADS_FOX)
        x2 = _out_proj(o_sb.reshape(n, -1), o_fx.reshape(n, -1), x2, sb_group_g[i:i + 1],
                       fox_group_g[i:i + 1], w(w_out[i]), mix_post_g[i:i + 1])
        x2 = _ffn(x2, ffn2_pre_g[i:i + 1], w(ffn2_w_gate[i]), w(ffn2_w_up[i]), w(ffn2_w_down[i]),
                  ffn2_post_g[i:i + 1])
        x2 = _ple(x2, p[i].reshape(n, -1), ple_pre_g[i:i + 1], w(w_ple_gate[i]), w(w_ple_proj[i]),
                  ple_post_g[i:i + 1])
    return x2.reshape(b, s, d)
```

```python
import functools

import jax
import jax.numpy as jnp
from jax import lax
from jax.experimental import pallas as pl
from jax.experimental.pallas import tpu as pltpu

HEAD_DIM = 64
N_HEADS_SB = 8
N_HEADS_FOX = 8
EPS = 1e-6
FFN_RES_WEIGHT = 0.5

LANES = 128
MXU_DTYPE = jnp.bfloat16
VMEM_LIMIT_BYTES = 48 * 1024 * 1024

FFN_ROWS = 512
FFN_COLS = 256
PROJ_ROWS = 512
ATTN_BLOCK = 256
HEADS_PER_STEP = 4
SB_EXIT = -105.0

F32 = jnp.float32
NEG_BIG = -1e30


def _dot(a, b):
    return jnp.dot(a, b, preferred_element_type=F32)


def _rms(x, g):
    return x * lax.rsqrt(jnp.mean(x * x, axis=-1, keepdims=True) + EPS) * g


def _split3(x):
    hi = x.astype(MXU_DTYPE).astype(F32)
    r = x - hi
    mid = r.astype(MXU_DTYPE).astype(F32)
    lo = (r - mid).astype(MXU_DTYPE).astype(F32)
    return hi, mid, lo


def _params(n_axes):
    return pltpu.CompilerParams(
        dimension_semantics=("arbitrary",) * n_axes,
        vmem_limit_bytes=VMEM_LIMIT_BYTES,
    )


def _ffn_kernel(x_ref, pre_g_ref, wg_ref, wu_ref, wd_ref, post_g_ref, o_ref, hn_ref, acc_ref):
    c = pl.program_id(1)

    @pl.when(c == 0)
    def _():
        hn_ref[...] = _rms(x_ref[...], pre_g_ref[...]).astype(MXU_DTYPE)
        acc_ref[...] = jnp.zeros_like(acc_ref)

    hn = hn_ref[...]
    g = _dot(hn, wg_ref[...])
    u = _dot(hn, wu_ref[...])
    act = (g * jax.nn.sigmoid(g)) * u
    acc_ref[...] += _dot(act.astype(MXU_DTYPE), wd_ref[...])

    @pl.when(c == pl.num_programs(1) - 1)
    def _():
        o_ref[...] = x_ref[...] + FFN_RES_WEIGHT * _rms(acc_ref[...], post_g_ref[...])


def _ffn(x2, pre_g, wg, wu, wd, post_g):
    n, d = x2.shape
    dff = wg.shape[1]
    tm = min(FFN_ROWS, n)
    tf = FFN_COLS
    return pl.pallas_call(
        _ffn_kernel,
        grid=(n // tm, dff // tf),
        in_specs=[
            pl.BlockSpec((tm, d), lambda i, c: (i, 0)),
            pl.BlockSpec((1, d), lambda i, c: (0, 0)),
            pl.BlockSpec((d, tf), lambda i, c: (0, c)),
            pl.BlockSpec((d, tf), lambda i, c: (0, c)),
            pl.BlockSpec((tf, d), lambda i, c: (c, 0)),
            pl.BlockSpec((1, d), lambda i, c: (0, 0)),
        ],
        out_specs=pl.BlockSpec((tm, d), lambda i, c: (i, 0)),
        out_shape=jax.ShapeDtypeStruct((n, d), F32),
        scratch_shapes=[pltpu.VMEM((tm, d), MXU_DTYPE), pltpu.VMEM((tm, d), F32)],
        compiler_params=_params(2),
        name="ffn",
    )(x2, pre_g, wg, wu, wd, post_g)


def _log_sigmoid(x):
    return jnp.minimum(x, 0.0) - jnp.log(1.0 + jnp.exp(-jnp.abs(x)))


def _inproj_kernel(x_ref, g_ref, wqkv_ref, wf_ref, bf_ref, q_ref, kt_ref, v_ref, carry_ref, *, tk):
    tm = x_ref.shape[1]
    d_sb = N_HEADS_SB * HEAD_DIM
    d_fx = N_HEADS_FOX * HEAD_DIM
    si = pl.program_id(1)

    @pl.when(si == 0)
    def _():
        carry_ref[...] = jnp.zeros_like(carry_ref)

    hn = _rms(x_ref[0], g_ref[...]).astype(MXU_DTYPE)

    ls = _log_sigmoid(_dot(hn, wf_ref[...]) + bf_ref[...])
    r = lax.broadcasted_iota(jnp.int32, (tm, tm), 0)
    c = lax.broadcasted_iota(jnp.int32, (tm, tm), 1)
    tri = (r >= c).astype(MXU_DTYPE)
    ls_hi, ls_mid, ls_lo = _split3(ls)
    fcum = (_dot(tri, ls_hi.astype(MXU_DTYPE)) + _dot(tri, ls_mid.astype(MXU_DTYPE))
            + _dot(tri, ls_lo.astype(MXU_DTYPE))) + carry_ref[...]
    carry_ref[...] = fcum[tm - 1:tm, :]

    lane = lax.broadcasted_iota(jnp.int32, (tm, LANES), 1)
    low = lane < HEAD_DIM

    def head_halves(col0):
        pr = _dot(hn, wqkv_ref[:, col0:col0 + LANES])
        return pr, pltpu.roll(pr, HEAD_DIM, 1)

    def store_k(h, k_aug):
        kt = k_aug.T
        for j in range(tm // tk):
            kt_ref[0, h, j] = kt[:, j * tk:(j + 1) * tk].astype(MXU_DTYPE)

    scale = HEAD_DIM ** -0.5
    for grp in range(d_sb // LANES):
        for half, pr in enumerate(head_halves(grp * LANES)):
            q_ref[0, 2 * grp + half] = jnp.where(low, pr * scale, 0.0).astype(MXU_DTYPE)
        for half, pr in enumerate(head_halves(d_sb + grp * LANES)):
            store_k(2 * grp + half, jnp.where(low, pr, 0.0))

    q0 = 3 * d_sb
    for grp in range(d_fx // LANES):
        qs = head_halves(q0 + grp * LANES)
        ks = head_halves(q0 + d_fx + grp * LANES)
        for half in range(2):
            hf = 2 * grp + half
            f_hi, f_mid, f_lo = _split3(fcum[:, hf:hf + 1])
            q_bias = jnp.where(lane == HEAD_DIM, f_hi,
                     jnp.where(lane == HEAD_DIM + 1, f_mid,
                     jnp.where(lane == HEAD_DIM + 2, f_lo,
                     jnp.where(lane < HEAD_DIM + 6, 1.0, 0.0))))
            k_bias = jnp.where(lane < HEAD_DIM + 3, 1.0,
                     jnp.where(lane == HEAD_DIM + 3, -f_hi,
                     jnp.where(lane == HEAD_DIM + 4, -f_mid,
                     jnp.where(lane == HEAD_DIM + 5, -f_lo, 0.0))))
            q_ref[0, N_HEADS_SB + hf] = jnp.where(low, qs[half] * scale, q_bias).astype(MXU_DTYPE)
            store_k(N_HEADS_SB + hf, jnp.where(low, ks[half], k_bias))

    v_ref[0, :, 0:d_sb] = _dot(hn, wqkv_ref[:, 2 * d_sb:3 * d_sb]).astype(MXU_DTYPE)
    v_ref[0, :, d_sb:d_sb + d_fx] = _dot(
        hn, wqkv_ref[:, q0 + 2 * d_fx:q0 + 3 * d_fx]).astype(MXU_DTYPE)


def _in_proj(x3, g, wqkv, wf, bf):
    b, s, d = x3.shape
    tm = min(PROJ_ROWS, s)
    tk = min(ATTN_BLOCK, s)
    nh = N_HEADS_SB + N_HEADS_FOX
    dv = (N_HEADS_SB + N_HEADS_FOX) * HEAD_DIM
    return pl.pallas_call(
        functools.partial(_inproj_kernel, tk=tk),
        grid=(b, s // tm),
        in_specs=[
            pl.BlockSpec((1, tm, d), lambda i, j: (i, j, 0)),
            pl.BlockSpec((1, d), lambda i, j: (0, 0)),
            pl.BlockSpec(wqkv.shape, lambda i, j: (0, 0)),
            pl.BlockSpec(wf.shape, lambda i, j: (0, 0)),
            pl.BlockSpec((1, LANES), lambda i, j: (0, 0)),
        ],
        out_specs=[
            pl.BlockSpec((1, nh, tm, LANES), lambda i, j: (i, 0, j, 0)),
            pl.BlockSpec((1, nh, tm // tk, LANES, tk), lambda i, j: (i, 0, j, 0, 0)),
            pl.BlockSpec((1, tm, dv), lambda i, j: (i, j, 0)),
        ],
        out_shape=[
            jax.ShapeDtypeStruct((b, nh, s, LANES), MXU_DTYPE),
            jax.ShapeDtypeStruct((b, nh, s // tk, LANES, tk), MXU_DTYPE),
            jax.ShapeDtypeStruct((b, s, dv), MXU_DTYPE),
        ],
        scratch_shapes=[pltpu.VMEM((1, LANES), F32)],
        compiler_params=_params(2),
        name="in_proj",
    )(x3, g, wqkv, wf, bf)


def _pair_lanes(h):
    return slice((h // 2) * LANES, (h // 2 + 1) * LANES)


def _merge_pairs(per_head, t):
    lane = lax.broadcasted_iota(jnp.int32, (t, LANES), 1)
    pairs = [jnp.where(lane < HEAD_DIM, per_head[h], per_head[h + 1])
             for h in range(0, len(per_head), 2)]
    return pairs[0] if len(pairs) == 1 else jnp.concatenate(pairs, axis=1)


def _sb_kernel(q_ref, kt_ref, v_ref, o_ref):
    g, t = q_ref.shape[1], q_ref.shape[2]
    qi = pl.program_id(2)
    row = lax.broadcasted_iota(jnp.int32, (t, t), 0)
    col = lax.broadcasted_iota(jnp.int32, (t, t), 1)
    strict = col < row
    jj = lax.broadcasted_iota(jnp.int32, (2 * t, t), 0)
    ss = lax.broadcasted_iota(jnp.int32, (2 * t, t), 1)
    suffix = (jnp.where(jj < t, jj, jj - t) >= ss).astype(MXU_DTYPE)

    def chunks(kc, state, diagonal):
        row0 = pl.ds(pl.multiple_of(kc * t, t), t)
        zs = [_dot(q_ref[0, h], kt_ref[0, h, kc]) for h in range(g)]
        cats = []
        for z in zs:
            sp = jnp.maximum(z, 0.0) + jnp.log(1.0 + jnp.exp(-jnp.abs(z)))
            if diagonal:
                sp = jnp.where(strict, sp, 0.0)
            hi = sp.astype(MXU_DTYPE)
            lo = (sp - hi.astype(F32)).astype(MXU_DTYPE)
            cats.append(jnp.concatenate([hi, lo], axis=1))
        csums = [_dot(cat, suffix) for cat in cats]
        ws = []
        for h in range(g):
            a = jnp.exp(zs[h] + state[h][0] - csums[h])
            if diagonal:
                a = jnp.where(strict, a, 0.0)
            ws.append(a.astype(MXU_DTYPE))
        return tuple((state[h][0] - csums[h][:, 0:1],
                      state[h][1] + _dot(ws[h], v_ref[0, row0, _pair_lanes(h)])) for h in range(g))

    def any_live(state):
        r_max = functools.reduce(jnp.maximum, [st[0] for st in state])
        return (jnp.max(r_max) > SB_EXIT).astype(jnp.int32)

    state = chunks(qi, ((jnp.zeros((t, 1), F32), jnp.zeros((t, LANES), F32)),) * g, True)

    def body(carry):
        i, _, st = carry
        st = chunks(qi - 1 - i, st, False)
        return i + 1, any_live(st), st

    _, _, state = lax.while_loop(lambda c: (c[0] < qi) & (c[1] > 0), body,
                                 (jnp.int32(0), any_live(state), state))
    o_ref[0] = _merge_pairs([st[1] for st in state], t)


def _fox_kernel(q_ref, kt_ref, v_ref, o_ref):
    g, t = q_ref.shape[1], q_ref.shape[2]
    qi = pl.program_id(2)
    row = lax.broadcasted_iota(jnp.int32, (t, t), 0)
    col = lax.broadcasted_iota(jnp.int32, (t, t), 1)
    causal = col <= row

    def chunks(kc, state, diagonal):
        row0 = pl.ds(pl.multiple_of(kc * t, t), t)
        ss = [_dot(q_ref[0, h], kt_ref[0, h, kc]) for h in range(g)]
        if diagonal:
            ss = [jnp.where(causal, s, NEG_BIG) for s in ss]
        stats, ps = [], []
        for h in range(g):
            m, l, _ = state[h]
            m_new = jnp.maximum(m, jnp.max(ss[h], axis=1, keepdims=True))
            alpha = jnp.exp(m - m_new)
            p = jnp.exp(ss[h] - m_new)
            stats.append((m_new, alpha * l + jnp.sum(p, axis=1, keepdims=True), alpha))
            ps.append(p.astype(MXU_DTYPE))
        return tuple((stats[h][0], stats[h][1],
                      stats[h][2] * state[h][2] + _dot(ps[h], v_ref[0, row0, _pair_lanes(h)]))
                     for h in range(g))

    init = (jnp.full((t, 1), NEG_BIG, F32), jnp.zeros((t, 1), F32), jnp.zeros((t, LANES), F32))
    state = chunks(qi, (init,) * g, True)
    state = lax.fori_loop(0, qi, lambda i, st: chunks(qi - 1 - i, st, False), state)
    o_ref[0] = _merge_pairs([st[2] / st[1] for st in state], t)


def _attention(body, name, q_aug, kt_aug, v, head0, n_heads):
    b, _, s, _ = q_aug.shape
    nk, t = kt_aug.shape[2], kt_aug.shape[4]
    g = HEADS_PER_STEP
    g0 = head0 // g
    gw = g * HEAD_DIM
    return pl.pallas_call(
        body,
        grid=(b, n_heads // g, s // t),
        in_specs=[
            pl.BlockSpec((1, g, t, LANES), lambda i, p, j: (i, g0 + p, j, 0)),
            pl.BlockSpec((1, g, nk, LANES, t), lambda i, p, j: (i, g0 + p, 0, 0, 0)),
            pl.BlockSpec((1, s, gw), lambda i, p, j: (i, 0, g0 + p)),
        ],
        out_specs=pl.BlockSpec((1, t, gw), lambda i, p, j: (i, j, p)),
        out_shape=jax.ShapeDtypeStruct((b, s, n_heads * HEAD_DIM), F32),
        compiler_params=_params(3),
        name=name,
    )(q_aug, kt_aug, v)


def _outproj_kernel(osb_ref, ofx_ref, x_ref, gsb_ref, gfx_ref, wout_ref, gpost_ref, o_ref):
    d_sb = osb_ref.shape[1]
    nsb = _rms(osb_ref[...], gsb_ref[...]).astype(MXU_DTYPE)
    nfx = _rms(ofx_ref[...], gfx_ref[...]).astype(MXU_DTYPE)
    m = _dot(nsb, wout_ref[0:d_sb, :]) + _dot(nfx, wout_ref[d_sb:, :])
    o_ref[...] = x_ref[...] + _rms(m, gpost_ref[...])


def _out_proj(o_sb, o_fx, x2, g_sb, g_fx, w_out, g_post):
    n, d = x2.shape
    tm = min(PROJ_ROWS, n)
    row = lambda w: pl.BlockSpec((tm, w), lambda i: (i, 0))
    full = lambda a: pl.BlockSpec(a.shape, lambda i: (0, 0))
    return pl.pallas_call(
        _outproj_kernel,
        grid=(n // tm,),
        in_specs=[row(o_sb.shape[1]), row(o_fx.shape[1]), row(d), full(g_sb), full(g_fx),
                  full(w_out), full(g_post)],
        out_specs=row(d),
        out_shape=jax.ShapeDtypeStruct((n, d), F32),
        compiler_params=_params(1),
        name="out_proj",
    )(o_sb, o_fx, x2, g_sb, g_fx, w_out, g_post)


def _ple_kernel(x_ref, p_ref, gpre_ref, wgate_ref, wproj_ref, gpost_ref, o_ref):
    x = x_ref[...]
    hn = _rms(x, gpre_ref[...]).astype(MXU_DTYPE)
    gate = jax.nn.sigmoid(_dot(hn, wgate_ref[...]))
    e = gate * _dot(p_ref[...].astype(MXU_DTYPE), wproj_ref[...])
    o_ref[...] = x + _rms(e, gpost_ref[...])


def _ple(x2, p2, g_pre, w_gate, w_proj, g_post):
    n, d = x2.shape
    tm = min(PROJ_ROWS, n)
    row = lambda w: pl.BlockSpec((tm, w), lambda i: (i, 0))
    full = lambda a: pl.BlockSpec(a.shape, lambda i: (0, 0))
    return pl.pallas_call(
        _ple_kernel,
        grid=(n // tm,),
        in_specs=[row(d), row(p2.shape[1]), full(g_pre), full(w_gate), full(w_proj), full(g_post)],
        out_specs=row(d),
        out_shape=jax.ShapeDtypeStruct((n, d), F32),
        compiler_params=_params(1),
        name="ple",
    )(x2, p2, g_pre, w_gate, w_proj, g_post)


def kernel(x, p, ffn1_pre_g, ffn1_w_gate, ffn1_w_up, ffn1_w_down, ffn1_post_g, mix_pre_g, w_in, b_forget, sb_group_g, fox_group_g, w_out, mix_post_g, ffn2_pre_g, ffn2_w_gate, ffn2_w_up, ffn2_w_down, ffn2_post_g, ple_pre_g, w_ple_gate, w_ple_proj, ple_post_g):
    b, s, d = x.shape
    depth = p.shape[0]
    n = b * s
    d_qkv = 3 * (N_HEADS_SB + N_HEADS_FOX) * HEAD_DIM
    w = lambda a: a.astype(MXU_DTYPE)
    x2 = x.reshape(n, d)
    for i in range(depth):
        x2 = _ffn(x2, ffn1_pre_g[i:i + 1], w(ffn1_w_gate[i]), w(ffn1_w_up[i]), w(ffn1_w_down[i]),
                  ffn1_post_g[i:i + 1])
        wf = jnp.pad(w_in[i][:, d_qkv:], ((0, 0), (0, LANES - N_HEADS_FOX)))
        bf = jnp.pad(b_forget[i:i + 1], ((0, 0), (0, LANES - N_HEADS_FOX)))
        q_aug, kt_aug, v = _in_proj(x2.reshape(b, s, d), mix_pre_g[i:i + 1], w(w_in[i][:, :d_qkv]),
                                    w(wf), bf)
        o_sb = _attention(_sb_kernel, "attn_sb", q_aug, kt_aug, v, 0, N_HEADS_SB)
        o_fx = _attention(_fox_kernel, "attn_fox", q_aug, kt_aug, v, N_HEADS_SB, N_HEADS_FOX)
        x2 = _out_proj(o_sb.reshape(n, -1), o_fx.reshape(n, -1), x2, sb_group_g[i:i + 1],
                       fox_group_g[i:i + 1], w(w_out[i]), mix_post_g[i:i + 1])
        x2 = _ffn(x2, ffn2_pre_g[i:i + 1], w(ffn2_w_gate[i]), w(ffn2_w_up[i]), w(ffn2_w_down[i]),
                  ffn2_post_g[i:i + 1])
        x2 = _ple(x2, p[i].reshape(n, -1), ple_pre_g[i:i + 1], w(w_ple_gate[i]), w(w_ple_proj[i]),
                  ple_post_g[i:i + 1])
    return x2.reshape(b, s, d)
```

```python
import functools

import jax
import jax.numpy as jnp
from jax import lax
from jax.experimental import pallas as pl
from jax.experimental.pallas import tpu as pltpu

HEAD_DIM = 64
N_HEADS_SB = 8
N_HEADS_FOX = 8
EPS = 1e-6
FFN_RES_WEIGHT = 0.5

LANES = 128
MXU_DTYPE = jnp.bfloat16
VMEM_LIMIT_BYTES = 48 * 1024 * 1024

FFN_ROWS = 512
FFN_COLS = 1408
PROJ_ROWS = 512
ATTN_BLOCK = 256
KEY_CHUNK = 256
HEADS_PER_STEP = 4
EXP_ZERO = -105.0
FOX_SKIP = EXP_ZERO - 1.0

F32 = jnp.float32
NEG_BIG = -1e30


def _dot(a, b):
    return jnp.dot(a, b, preferred_element_type=F32)


def _rms(x, g):
    return x * lax.rsqrt(jnp.mean(x * x, axis=-1, keepdims=True) + EPS) * g


def _split3(x):
    hi = x.astype(MXU_DTYPE).astype(F32)
    r = x - hi
    mid = r.astype(MXU_DTYPE).astype(F32)
    lo = (r - mid).astype(MXU_DTYPE).astype(F32)
    return hi, mid, lo


def _params(n_axes):
    return pltpu.CompilerParams(
        dimension_semantics=("arbitrary",) * n_axes,
        vmem_limit_bytes=VMEM_LIMIT_BYTES,
    )


def _ffn_kernel(x_ref, pre_g_ref, wg_ref, wu_ref, wd_ref, post_g_ref, o_ref, hn_ref, acc_ref):
    c = pl.program_id(1)

    @pl.when(c == 0)
    def _():
        hn_ref[...] = _rms(x_ref[...], pre_g_ref[...]).astype(MXU_DTYPE)
        acc_ref[...] = jnp.zeros_like(acc_ref)

    hn = hn_ref[...]
    g = _dot(hn, wg_ref[...])
    u = _dot(hn, wu_ref[...])
    act = (g * jax.nn.sigmoid(g)) * u
    acc_ref[...] += _dot(act.astype(MXU_DTYPE), wd_ref[...])

    @pl.when(c == pl.num_programs(1) - 1)
    def _():
        o_ref[...] = x_ref[...] + FFN_RES_WEIGHT * _rms(acc_ref[...], post_g_ref[...])


def _ffn(x2, pre_g, wg, wu, wd, post_g):
    n, d = x2.shape
    dff = wg.shape[1]
    tm = min(FFN_ROWS, n)
    tf = FFN_COLS
    return pl.pallas_call(
        _ffn_kernel,
        grid=(n // tm, dff // tf),
        in_specs=[
            pl.BlockSpec((tm, d), lambda i, c: (i, 0)),
            pl.BlockSpec((1, d), lambda i, c: (0, 0)),
            pl.BlockSpec((d, tf), lambda i, c: (0, c)),
            pl.BlockSpec((d, tf), lambda i, c: (0, c)),
            pl.BlockSpec((tf, d), lambda i, c: (c, 0)),
            pl.BlockSpec((1, d), lambda i, c: (0, 0)),
        ],
        out_specs=pl.BlockSpec((tm, d), lambda i, c: (i, 0)),
        out_shape=jax.ShapeDtypeStruct((n, d), F32),
        scratch_shapes=[pltpu.VMEM((tm, d), MXU_DTYPE), pltpu.VMEM((tm, d), F32)],
        compiler_params=_params(2),
        name="ffn",
    )(x2, pre_g, wg, wu, wd, post_g)


def _log_sigmoid(x):
    return jnp.minimum(x, 0.0) - jnp.log(1.0 + jnp.exp(-jnp.abs(x)))


def _inproj_kernel(x_ref, g_ref, wqkv_ref, wf_ref, bf_ref,
                   qt_ref, k_ref, vt_ref, fend_ref, kn2_ref, carry_ref, *, tq, tk):
    tm = x_ref.shape[1]
    d_sb = N_HEADS_SB * HEAD_DIM
    d_fx = N_HEADS_FOX * HEAD_DIM
    si = pl.program_id(1)

    @pl.when(si == 0)
    def _():
        carry_ref[...] = jnp.zeros_like(carry_ref)

    hn = _rms(x_ref[0], g_ref[...]).astype(MXU_DTYPE)

    ls = _log_sigmoid(_dot(hn, wf_ref[...]) + bf_ref[...])
    r = lax.broadcasted_iota(jnp.int32, (tm, tm), 0)
    c = lax.broadcasted_iota(jnp.int32, (tm, tm), 1)
    tri = (r >= c).astype(MXU_DTYPE)
    ls_hi, ls_mid, ls_lo = _split3(ls)
    fcum = (_dot(tri, ls_hi.astype(MXU_DTYPE)) + _dot(tri, ls_mid.astype(MXU_DTYPE))
            + _dot(tri, ls_lo.astype(MXU_DTYPE))) + carry_ref[...]
    carry_ref[...] = fcum[tm - 1:tm, :]
    for j in range(tm // tk):
        fend_ref[0, 0, j:j + 1, :] = fcum[(j + 1) * tk - 1:(j + 1) * tk, :]

    lane = lax.broadcasted_iota(jnp.int32, (tm, LANES), 1)
    lane1 = lax.broadcasted_iota(jnp.int32, (1, LANES), 1)
    low = lane < HEAD_DIM

    def head_halves(col0):
        pr = _dot(hn, wqkv_ref[:, col0:col0 + LANES])
        return pr, pltpu.roll(pr, HEAD_DIM, 1)

    def store_q(h, q_aug):
        qt = q_aug.T
        for j in range(tm // tq):
            qt_ref[0, h, j] = qt[:, j * tq:(j + 1) * tq].astype(MXU_DTYPE)

    scale = HEAD_DIM ** -0.5
    for grp in range(d_sb // LANES):
        for half, pr in enumerate(head_halves(grp * LANES)):
            store_q(2 * grp + half, jnp.where(low, pr * scale, 0.0))
        for half, pr in enumerate(head_halves(d_sb + grp * LANES)):
            k_ref[0, 2 * grp + half] = jnp.where(low, pr, 0.0).astype(MXU_DTYPE)

    q0 = 3 * d_sb
    kn2 = jnp.zeros((1, LANES), F32)
    for grp in range(d_fx // LANES):
        qs = head_halves(q0 + grp * LANES)
        ks = head_halves(q0 + d_fx + grp * LANES)
        for half in range(2):
            hf = 2 * grp + half
            f_hi, f_mid, f_lo = _split3(fcum[:, hf:hf + 1])
            q_bias = jnp.where(lane == HEAD_DIM, f_hi,
                     jnp.where(lane == HEAD_DIM + 1, f_mid,
                     jnp.where(lane == HEAD_DIM + 2, f_lo,
                     jnp.where(lane < HEAD_DIM + 6, 1.0, 0.0))))
            k_bias = jnp.where(lane < HEAD_DIM + 3, 1.0,
                     jnp.where(lane == HEAD_DIM + 3, -f_hi,
                     jnp.where(lane == HEAD_DIM + 4, -f_mid,
                     jnp.where(lane == HEAD_DIM + 5, -f_lo, 0.0))))
            store_q(N_HEADS_SB + hf, jnp.where(low, qs[half] * scale, q_bias))
            k_used = jnp.where(low, ks[half], 0.0).astype(MXU_DTYPE).astype(F32)
            norm2 = jnp.max(jnp.sum(k_used * k_used, axis=1, keepdims=True), axis=0, keepdims=True)
            kn2 = jnp.where(lane1 == hf, norm2, kn2)
            k_ref[0, N_HEADS_SB + hf] = jnp.where(low, ks[half], k_bias).astype(MXU_DTYPE)
    kn2_ref[0, 0] = kn2

    def store_v(p0, col0, width):
        for grp in range(width // LANES):
            vt = _dot(hn, wqkv_ref[:, col0 + grp * LANES:col0 + (grp + 1) * LANES]).T
            for j in range(tm // tk):
                vt_ref[0, p0 + grp, j] = vt[:, j * tk:(j + 1) * tk].astype(MXU_DTYPE)

    store_v(0, 2 * d_sb, d_sb)
    store_v(d_sb // LANES, q0 + 2 * d_fx, d_fx)


def _in_proj(x3, g, wqkv, wf, bf):
    b, s, d = x3.shape
    tm = min(PROJ_ROWS, s)
    tq = min(ATTN_BLOCK, s)
    tk = min(KEY_CHUNK, s)
    nh = N_HEADS_SB + N_HEADS_FOX
    return pl.pallas_call(
        functools.partial(_inproj_kernel, tq=tq, tk=tk),
        grid=(b, s // tm),
        in_specs=[
            pl.BlockSpec((1, tm, d), lambda i, j: (i, j, 0)),
            pl.BlockSpec((1, d), lambda i, j: (0, 0)),
            pl.BlockSpec(wqkv.shape, lambda i, j: (0, 0)),
            pl.BlockSpec(wf.shape, lambda i, j: (0, 0)),
            pl.BlockSpec((1, LANES), lambda i, j: (0, 0)),
        ],
        out_specs=[
            pl.BlockSpec((1, nh, tm // tq, LANES, tq), lambda i, j: (i, 0, j, 0, 0)),
            pl.BlockSpec((1, nh, tm, LANES), lambda i, j: (i, 0, j, 0)),
            pl.BlockSpec((1, nh // 2, tm // tk, LANES, tk), lambda i, j: (i, 0, j, 0, 0)),
            pl.BlockSpec((1, 1, tm // tk, LANES), lambda i, j: (i, j, 0, 0)),
            pl.BlockSpec((1, 1, 1, LANES), lambda i, j: (i, j, 0, 0)),
        ],
        out_shape=[
            jax.ShapeDtypeStruct((b, nh, s // tq, LANES, tq), MXU_DTYPE),
            jax.ShapeDtypeStruct((b, nh, s, LANES), MXU_DTYPE),
            jax.ShapeDtypeStruct((b, nh // 2, s // tk, LANES, tk), MXU_DTYPE),
            jax.ShapeDtypeStruct((b, s // tm, tm // tk, LANES), F32),
            jax.ShapeDtypeStruct((b, s // tm, 1, LANES), F32),
        ],
        scratch_shapes=[pltpu.VMEM((1, LANES), F32)],
        compiler_params=_params(2),
        name="in_proj",
    )(x3, g, wqkv, wf, bf)


def _key_query_iota(tk, tq):
    return (lax.broadcasted_iota(jnp.int32, (tk, tq), 0), lax.broadcasted_iota(jnp.int32, (tk, tq), 1))


def _store_heads(o_ref, acc_t):
    for p in range(len(acc_t) // 2):
        pair_t = jnp.concatenate([acc_t[2 * p], acc_t[2 * p + 1]], axis=0)
        o_ref[0, :, p * LANES:(p + 1) * LANES] = pair_t.T


def _v_t(vt_ref, h, kc):
    half = (h % 2) * HEAD_DIM
    return vt_ref[0, h // 2, kc, half:half + HEAD_DIM, :]


def _sb_kernel(qt_ref, k_ref, vt_ref, o_ref):
    g, tq = qt_ref.shape[1], qt_ref.shape[4]
    tk = vt_ref.shape[4]
    n_diag = tq // tk
    qi = pl.program_id(2)
    key, query = _key_query_iota(tk, tq)
    ss = lax.broadcasted_iota(jnp.int32, (tk, 2 * tk), 0)
    jj = lax.broadcasted_iota(jnp.int32, (tk, 2 * tk), 1)
    suffix_t = (jnp.where(jj < tk, jj, jj - tk) >= ss).astype(MXU_DTYPE)

    def chunks(kc, state, diag):
        rows = pl.ds(pl.multiple_of(kc * tk, tk), tk)
        keep = None if diag is None else (key + diag * tk) < query
        zs = [_dot(k_ref[0, h, rows, :], qt_ref[0, h, 0]) for h in range(g)]
        stacks = []
        for z in zs:
            sp = jnp.maximum(z, 0.0) + jnp.log(1.0 + jnp.exp(-jnp.abs(z)))
            if keep is not None:
                sp = jnp.where(keep, sp, 0.0)
            hi = sp.astype(MXU_DTYPE)
            lo = (sp - hi.astype(F32)).astype(MXU_DTYPE)
            stacks.append(jnp.concatenate([hi, lo], axis=0))
        csums = [_dot(suffix_t, st) for st in stacks]
        ws = []
        for h in range(g):
            a = jnp.exp(zs[h] + state[h][0] - csums[h])
            if keep is not None:
                a = jnp.where(keep, a, 0.0)
            ws.append(a.astype(MXU_DTYPE))
        return tuple((state[h][0] - csums[h][0:1, :],
                      state[h][1] + _dot(_v_t(vt_ref, h, kc), ws[h])) for h in range(g))

    def any_live(state):
        r_max = functools.reduce(jnp.maximum, [st[0] for st in state])
        return (jnp.max(r_max) > EXP_ZERO).astype(jnp.int32)

    state = ((jnp.zeros((1, tq), F32), jnp.zeros((HEAD_DIM, tq), F32)),) * g
    for d in reversed(range(n_diag)):
        state = chunks(qi * n_diag + d, state, d)

    def body(carry):
        i, _, st = carry
        st = chunks(qi * n_diag - 1 - i, st, None)
        return i + 1, any_live(st), st

    _, _, state = lax.while_loop(lambda c: (c[0] < qi * n_diag) & (c[1] > 0), body,
                                 (jnp.int32(0), any_live(state), state))
    _store_heads(o_ref, [st[1] for st in state])


def _fox_kernel(fend_ref, kn_ref, qt_ref, k_ref, vt_ref, o_ref):
    g, tq = qt_ref.shape[1], qt_ref.shape[4]
    tk = vt_ref.shape[4]
    n_diag = tq // tk
    n_chunks = k_ref.shape[2] // tk
    bi, gi, qi = pl.program_id(0), pl.program_id(1), pl.program_id(2)
    key, query = _key_query_iota(tk, tq)

    def chunks(kc, state, diag):
        rows = pl.ds(pl.multiple_of(kc * tk, tk), tk)
        keep = None if diag is None else (key + diag * tk) <= query
        ss = [_dot(k_ref[0, h, rows, :], qt_ref[0, h, 0]) for h in range(g)]
        if keep is not None:
            ss = [jnp.where(keep, s, NEG_BIG) for s in ss]
        stats, ps = [], []
        for h in range(g):
            m, l, _ = state[h]
            m_new = jnp.maximum(m, jnp.max(ss[h], axis=0, keepdims=True))
            alpha = jnp.exp(m - m_new)
            p = jnp.exp(ss[h] - m_new)
            if keep is not None:
                p = jnp.where(keep, p, 0.0)
            stats.append((m_new, alpha * l + jnp.sum(p, axis=0, keepdims=True), alpha))
            ps.append(p.astype(MXU_DTYPE))
        return tuple((stats[h][0], stats[h][1],
                      stats[h][2] * state[h][2] + _dot(_v_t(vt_ref, h, kc), ps[h]))
                     for h in range(g))

    init = (jnp.full((1, tq), NEG_BIG, F32), jnp.zeros((1, tq), F32), jnp.zeros((HEAD_DIM, tq), F32))
    state = (init,) * g
    for d in range(n_diag):
        state = chunks(qi * n_diag + d, state, d)

    heads = [(bi * (N_HEADS_FOX) + gi * g + h) for h in range(g)]
    slack = []
    for h in range(g):
        qt = qt_ref[0, h, 0].astype(F32)
        qn = jnp.sqrt(jnp.sum(qt[0:HEAD_DIM] * qt[0:HEAD_DIM], axis=0, keepdims=True))
        f_t = jnp.sum(qt[HEAD_DIM:HEAD_DIM + 8], axis=0, keepdims=True) - 3.0
        bound = qn * (kn_ref[heads[h]] * 1.001) + f_t - state[h][0]
        slack.append(jnp.max(bound))

    def any_live(kc):
        kc = jnp.maximum(kc, 0)
        live = [slack[h] - fend_ref[heads[h] * n_chunks + kc] > FOX_SKIP for h in range(g)]
        return functools.reduce(jnp.logical_or, live).astype(jnp.int32)

    def body(carry):
        i, _, st = carry
        kc = qi * n_diag - 1 - i
        return i + 1, any_live(kc - 1), chunks(kc, st, None)

    _, _, state = lax.while_loop(lambda c: (c[0] < qi * n_diag) & (c[1] > 0), body,
                                 (jnp.int32(0), any_live(qi * n_diag - 1), state))
    _store_heads(o_ref, [st[2] / st[1] for st in state])


def _attention(body, name, qt_aug, k_aug, vt, head0, n_heads, prefetch=()):
    b, _, nq, _, tq = qt_aug.shape
    s = k_aug.shape[2]
    nk, tk = vt.shape[2], vt.shape[4]
    g = HEADS_PER_STEP
    g0 = head0 // g
    return pl.pallas_call(
        body,
        grid_spec=pltpu.PrefetchScalarGridSpec(
            num_scalar_prefetch=len(prefetch),
            grid=(b, n_heads // g, nq),
            in_specs=[
                pl.BlockSpec((1, g, 1, LANES, tq), lambda i, p, j, *_: (i, g0 + p, j, 0, 0)),
                pl.BlockSpec((1, g, s, LANES), lambda i, p, j, *_: (i, g0 + p, 0, 0)),
                pl.BlockSpec((1, g // 2, nk, LANES, tk), lambda i, p, j, *_: (i, g0 + p, 0, 0, 0)),
            ],
            out_specs=pl.BlockSpec((1, tq, g * HEAD_DIM), lambda i, p, j, *_: (i, j, p)),
        ),
        out_shape=jax.ShapeDtypeStruct((b, nq * tq, n_heads * HEAD_DIM), F32),
        compiler_params=_params(3),
        name=name,
    )(*prefetch, qt_aug, k_aug, vt)


def _outproj_kernel(osb_ref, ofx_ref, x_ref, gsb_ref, gfx_ref, wout_ref, gpost_ref, o_ref):
    d_sb = osb_ref.shape[1]
    nsb = _rms(osb_ref[...], gsb_ref[...]).astype(MXU_DTYPE)
    nfx = _rms(ofx_ref[...], gfx_ref[...]).astype(MXU_DTYPE)
    m = _dot(nsb, wout_ref[0:d_sb, :]) + _dot(nfx, wout_ref[d_sb:, :])
    o_ref[...] = x_ref[...] + _rms(m, gpost_ref[...])


def _out_proj(o_sb, o_fx, x2, g_sb, g_fx, w_out, g_post):
    n, d = x2.shape
    tm = min(PROJ_ROWS, n)
    row = lambda w: pl.BlockSpec((tm, w), lambda i: (i, 0))
    full = lambda a: pl.BlockSpec(a.shape, lambda i: (0, 0))
    return pl.pallas_call(
        _outproj_kernel,
        grid=(n // tm,),
        in_specs=[row(o_sb.shape[1]), row(o_fx.shape[1]), row(d), full(g_sb), full(g_fx),
                  full(w_out), full(g_post)],
        out_specs=row(d),
        out_shape=jax.ShapeDtypeStruct((n, d), F32),
        compiler_params=_params(1),
        name="out_proj",
    )(o_sb, o_fx, x2, g_sb, g_fx, w_out, g_post)


def _ple_kernel(x_ref, p_ref, gpre_ref, wgate_ref, wproj_ref, gpost_ref, o_ref):
    x = x_ref[...]
    hn = _rms(x, gpre_ref[...]).astype(MXU_DTYPE)
    gate = jax.nn.sigmoid(_dot(hn, wgate_ref[...]))
    e = gate * _dot(p_ref[...].astype(MXU_DTYPE), wproj_ref[...])
    o_ref[...] = x + _rms(e, gpost_ref[...])


def _ple(x2, p2, g_pre, w_gate, w_proj, g_post):
    n, d = x2.shape
    tm = min(PROJ_ROWS, n)
    row = lambda w: pl.BlockSpec((tm, w), lambda i: (i, 0))
    full = lambda a: pl.BlockSpec(a.shape, lambda i: (0, 0))
    return pl.pallas_call(
        _ple_kernel,
        grid=(n // tm,),
        in_specs=[row(d), row(p2.shape[1]), full(g_pre), full(w_gate), full(w_proj), full(g_post)],
        out_specs=row(d),
        out_shape=jax.ShapeDtypeStruct((n, d), F32),
        compiler_params=_params(1),
        name="ple",
    )(x2, p2, g_pre, w_gate, w_proj, g_post)


def kernel(x, p, ffn1_pre_g, ffn1_w_gate, ffn1_w_up, ffn1_w_down, ffn1_post_g, mix_pre_g, w_in, b_forget, sb_group_g, fox_group_g, w_out, mix_post_g, ffn2_pre_g, ffn2_w_gate, ffn2_w_up, ffn2_w_down, ffn2_post_g, ple_pre_g, w_ple_gate, w_ple_proj, ple_post_g):
    b, s, d = x.shape
    depth = p.shape[0]
    n = b * s
    d_qkv = 3 * (N_HEADS_SB + N_HEADS_FOX) * HEAD_DIM
    w = lambda a: a.astype(MXU_DTYPE)
    x2 = x.reshape(n, d)
    for i in range(depth):
        x2 = _ffn(x2, ffn1_pre_g[i:i + 1], w(ffn1_w_gate[i]), w(ffn1_w_up[i]), w(ffn1_w_down[i]),
                  ffn1_post_g[i:i + 1])
        wf = jnp.pad(w_in[i][:, d_qkv:], ((0, 0), (0, LANES - N_HEADS_FOX)))
        bf = jnp.pad(b_forget[i:i + 1], ((0, 0), (0, LANES - N_HEADS_FOX)))
        qt_aug, k_aug, vt, fend, kn2 = _in_proj(x2.reshape(b, s, d), mix_pre_g[i:i + 1],
                                                w(w_in[i][:, :d_qkv]), w(wf), bf)
        fend = fend.reshape(b, -1, LANES)[:, :, :N_HEADS_FOX].transpose(0, 2, 1).reshape(-1)
        kn = jnp.sqrt(jnp.max(kn2[:, :, 0, :N_HEADS_FOX], axis=1)).reshape(-1)
        o_sb = _attention(_sb_kernel, "attn_sb", qt_aug, k_aug, vt, 0, N_HEADS_SB)
        o_fx = _attention(_fox_kernel, "attn_fox", qt_aug, k_aug,
                          vt, N_HEADS_SB, N_HEADS_FOX, prefetch=(fend, kn))
        x2 = _out_proj(o_sb.reshape(n, -1), o_fx.reshape(n, -1), x2, sb_group_g[i:i + 1],
                       fox_group_g[i:i + 1], w(w_out[i]), mix_post_g[i:i + 1])
        x2 = _ffn(x2, ffn2_pre_g[i:i + 1], w(ffn2_w_gate[i]), w(ffn2_w_up[i]), w(ffn2_w_down[i]),
                  ffn2_post_g[i:i + 1])
        x2 = _ple(x2, p[i].reshape(n, -1), ple_pre_g[i:i + 1], w(w_ple_gate[i]), w(w_ple_proj[i]),
                  ple_post_g[i:i + 1])
    return x2.reshape(b, s, d)
```

```python
import functools

import jax
import jax.numpy as jnp
from jax import lax
from jax.experimental import pallas as pl
from jax.experimental.pallas import tpu as pltpu

HEAD_DIM = 64
N_HEADS_SB = 8
N_HEADS_FOX = 8
EPS = 1e-6
FFN_RES_WEIGHT = 0.5

LANES = 128
MXU_DTYPE = jnp.bfloat16
VMEM_LIMIT_BYTES = 48 * 1024 * 1024

FFN_ROWS = 512
FFN_COLS = 1408
PROJ_ROWS = 512
ATTN_BLOCK = 256
KEY_CHUNK = 256
SB_HEADS, SB_QTILES = 4, 1
FOX_HEADS, FOX_QTILES = 2, 4
EXP_ZERO = -105.0
FOX_SKIP = EXP_ZERO - 1.0

F32 = jnp.float32
NEG_BIG = -1e30


def _dot(a, b):
    return jnp.dot(a, b, preferred_element_type=F32)


def _rms(x, g):
    return x * lax.rsqrt(jnp.mean(x * x, axis=-1, keepdims=True) + EPS) * g


def _split3(x):
    hi = x.astype(MXU_DTYPE).astype(F32)
    r = x - hi
    mid = r.astype(MXU_DTYPE).astype(F32)
    lo = (r - mid).astype(MXU_DTYPE).astype(F32)
    return hi, mid, lo


def _params(n_axes):
    return pltpu.CompilerParams(
        dimension_semantics=("arbitrary",) * n_axes,
        vmem_limit_bytes=VMEM_LIMIT_BYTES,
    )


def _ffn_kernel(x_ref, pre_g_ref, wg_ref, wu_ref, wd_ref, post_g_ref, o_ref, hn_ref, acc_ref):
    c = pl.program_id(1)

    @pl.when(c == 0)
    def _():
        hn_ref[...] = _rms(x_ref[...], pre_g_ref[...]).astype(MXU_DTYPE)
        acc_ref[...] = jnp.zeros_like(acc_ref)

    hn = hn_ref[...]
    g = _dot(hn, wg_ref[...])
    u = _dot(hn, wu_ref[...])
    act = (g * jax.nn.sigmoid(g)) * u
    acc_ref[...] += _dot(act.astype(MXU_DTYPE), wd_ref[...])

    @pl.when(c == pl.num_programs(1) - 1)
    def _():
        o_ref[...] = x_ref[...] + FFN_RES_WEIGHT * _rms(acc_ref[...], post_g_ref[...])


def _ffn(x2, pre_g, wg, wu, wd, post_g):
    n, d = x2.shape
    dff = wg.shape[1]
    tm = min(FFN_ROWS, n)
    tf = FFN_COLS
    return pl.pallas_call(
        _ffn_kernel,
        grid=(n // tm, dff // tf),
        in_specs=[
            pl.BlockSpec((tm, d), lambda i, c: (i, 0)),
            pl.BlockSpec((1, d), lambda i, c: (0, 0)),
            pl.BlockSpec((d, tf), lambda i, c: (0, c)),
            pl.BlockSpec((d, tf), lambda i, c: (0, c)),
            pl.BlockSpec((tf, d), lambda i, c: (c, 0)),
            pl.BlockSpec((1, d), lambda i, c: (0, 0)),
        ],
        out_specs=pl.BlockSpec((tm, d), lambda i, c: (i, 0)),
        out_shape=jax.ShapeDtypeStruct((n, d), F32),
        scratch_shapes=[pltpu.VMEM((tm, d), MXU_DTYPE), pltpu.VMEM((tm, d), F32)],
        compiler_params=_params(2),
        name="ffn",
    )(x2, pre_g, wg, wu, wd, post_g)


def _log_sigmoid(x):
    return jnp.minimum(x, 0.0) - jnp.log(1.0 + jnp.exp(-jnp.abs(x)))


def _inproj_kernel(x_ref, g_ref, wqkv_ref, wf_ref, bf_ref,
                   qt_ref, k_ref, vt_ref, fend_ref, kn2_ref, carry_ref, *, tq, tk):
    tm = x_ref.shape[1]
    d_sb = N_HEADS_SB * HEAD_DIM
    d_fx = N_HEADS_FOX * HEAD_DIM
    si = pl.program_id(1)

    @pl.when(si == 0)
    def _():
        carry_ref[...] = jnp.zeros_like(carry_ref)

    hn = _rms(x_ref[0], g_ref[...]).astype(MXU_DTYPE)

    ls = _log_sigmoid(_dot(hn, wf_ref[...]) + bf_ref[...])
    r = lax.broadcasted_iota(jnp.int32, (tm, tm), 0)
    c = lax.broadcasted_iota(jnp.int32, (tm, tm), 1)
    tri = (r >= c).astype(MXU_DTYPE)
    ls_hi, ls_mid, ls_lo = _split3(ls)
    fcum = (_dot(tri, ls_hi.astype(MXU_DTYPE)) + _dot(tri, ls_mid.astype(MXU_DTYPE))
            + _dot(tri, ls_lo.astype(MXU_DTYPE))) + carry_ref[...]
    carry_ref[...] = fcum[tm - 1:tm, :]
    for j in range(tm // tk):
        fend_ref[0, 0, j:j + 1, :] = fcum[(j + 1) * tk - 1:(j + 1) * tk, :]

    lane = lax.broadcasted_iota(jnp.int32, (tm, LANES), 1)
    lane1 = lax.broadcasted_iota(jnp.int32, (1, LANES), 1)
    low = lane < HEAD_DIM

    def head_halves(col0):
        pr = _dot(hn, wqkv_ref[:, col0:col0 + LANES])
        return pr, pltpu.roll(pr, HEAD_DIM, 1)

    def store_q(h, q_aug):
        qt = q_aug.T
        for j in range(tm // tq):
            qt_ref[0, h, j] = qt[:, j * tq:(j + 1) * tq].astype(MXU_DTYPE)

    scale = HEAD_DIM ** -0.5
    for grp in range(d_sb // LANES):
        for half, pr in enumerate(head_halves(grp * LANES)):
            store_q(2 * grp + half, jnp.where(low, pr * scale, 0.0))
        for half, pr in enumerate(head_halves(d_sb + grp * LANES)):
            k_ref[0, 2 * grp + half] = jnp.where(low, pr, 0.0).astype(MXU_DTYPE)

    q0 = 3 * d_sb
    kn2 = jnp.zeros((1, LANES), F32)
    for grp in range(d_fx // LANES):
        qs = head_halves(q0 + grp * LANES)
        ks = head_halves(q0 + d_fx + grp * LANES)
        for half in range(2):
            hf = 2 * grp + half
            f_hi, f_mid, f_lo = _split3(fcum[:, hf:hf + 1])
            q_bias = jnp.where(lane == HEAD_DIM, f_hi,
                     jnp.where(lane == HEAD_DIM + 1, f_mid,
                     jnp.where(lane == HEAD_DIM + 2, f_lo,
                     jnp.where(lane < HEAD_DIM + 6, 1.0, 0.0))))
            k_bias = jnp.where(lane < HEAD_DIM + 3, 1.0,
                     jnp.where(lane == HEAD_DIM + 3, -f_hi,
                     jnp.where(lane == HEAD_DIM + 4, -f_mid,
                     jnp.where(lane == HEAD_DIM + 5, -f_lo, 0.0))))
            store_q(N_HEADS_SB + hf, jnp.where(low, qs[half] * scale, q_bias))
            k_used = jnp.where(low, ks[half], 0.0).astype(MXU_DTYPE).astype(F32)
            norm2 = jnp.max(jnp.sum(k_used * k_used, axis=1, keepdims=True), axis=0, keepdims=True)
            kn2 = jnp.where(lane1 == hf, norm2, kn2)
            k_ref[0, N_HEADS_SB + hf] = jnp.where(low, ks[half], k_bias).astype(MXU_DTYPE)
    kn2_ref[0, 0] = kn2

    def store_v(p0, col0, width):
        for grp in range(width // LANES):
            vt = _dot(hn, wqkv_ref[:, col0 + grp * LANES:col0 + (grp + 1) * LANES]).T
            for j in range(tm // tk):
                vt_ref[0, p0 + grp, j] = vt[:, j * tk:(j + 1) * tk].astype(MXU_DTYPE)

    store_v(0, 2 * d_sb, d_sb)
    store_v(d_sb // LANES, q0 + 2 * d_fx, d_fx)


def _in_proj(x3, g, wqkv, wf, bf):
    b, s, d = x3.shape
    tm = min(PROJ_ROWS, s)
    tq = min(ATTN_BLOCK, s)
    tk = min(KEY_CHUNK, s)
    nh = N_HEADS_SB + N_HEADS_FOX
    return pl.pallas_call(
        functools.partial(_inproj_kernel, tq=tq, tk=tk),
        grid=(b, s // tm),
        in_specs=[
            pl.BlockSpec((1, tm, d), lambda i, j: (i, j, 0)),
            pl.BlockSpec((1, d), lambda i, j: (0, 0)),
            pl.BlockSpec(wqkv.shape, lambda i, j: (0, 0)),
            pl.BlockSpec(wf.shape, lambda i, j: (0, 0)),
            pl.BlockSpec((1, LANES), lambda i, j: (0, 0)),
        ],
        out_specs=[
            pl.BlockSpec((1, nh, tm // tq, LANES, tq), lambda i, j: (i, 0, j, 0, 0)),
            pl.BlockSpec((1, nh, tm, LANES), lambda i, j: (i, 0, j, 0)),
            pl.BlockSpec((1, nh // 2, tm // tk, LANES, tk), lambda i, j: (i, 0, j, 0, 0)),
            pl.BlockSpec((1, 1, tm // tk, LANES), lambda i, j: (i, j, 0, 0)),
            pl.BlockSpec((1, 1, 1, LANES), lambda i, j: (i, j, 0, 0)),
        ],
        out_shape=[
            jax.ShapeDtypeStruct((b, nh, s // tq, LANES, tq), MXU_DTYPE),
            jax.ShapeDtypeStruct((b, nh, s, LANES), MXU_DTYPE),
            jax.ShapeDtypeStruct((b, nh // 2, s // tk, LANES, tk), MXU_DTYPE),
            jax.ShapeDtypeStruct((b, s // tm, tm // tk, LANES), F32),
            jax.ShapeDtypeStruct((b, s // tm, 1, LANES), F32),
        ],
        scratch_shapes=[pltpu.VMEM((1, LANES), F32)],
        compiler_params=_params(2),
        name="in_proj",
    )(x3, g, wqkv, wf, bf)


def _key_query_iota(tk, tq):
    return (lax.broadcasted_iota(jnp.int32, (tk, tq), 0), lax.broadcasted_iota(jnp.int32, (tk, tq), 1))


def _q_t(qt_ref, h):
    tiles = [qt_ref[0, h, j] for j in range(qt_ref.shape[2])]
    return tiles[0] if len(tiles) == 1 else jnp.concatenate(tiles, axis=1)


def _store_heads(o_ref, acc_t):
    for p in range(len(acc_t) // 2):
        pair_t = jnp.concatenate([acc_t[2 * p], acc_t[2 * p + 1]], axis=0)
        o_ref[0, :, p * LANES:(p + 1) * LANES] = pair_t.T


def _v_t(vt_ref, h, kc):
    half = (h % 2) * HEAD_DIM
    return vt_ref[0, h // 2, kc, half:half + HEAD_DIM, :]


def _sb_kernel(qt_ref, k_ref, vt_ref, o_ref):
    g, tq = qt_ref.shape[1], qt_ref.shape[2] * qt_ref.shape[4]
    tk = vt_ref.shape[4]
    n_diag = tq // tk
    qi = pl.program_id(2)
    key, query = _key_query_iota(tk, tq)
    ss = lax.broadcasted_iota(jnp.int32, (tk, 2 * tk), 0)
    jj = lax.broadcasted_iota(jnp.int32, (tk, 2 * tk), 1)
    suffix_t = (jnp.where(jj < tk, jj, jj - tk) >= ss).astype(MXU_DTYPE)

    def chunks(kc, state, diag):
        rows = pl.ds(pl.multiple_of(kc * tk, tk), tk)
        keep = None if diag is None else (key + diag * tk) < query
        zs = [_dot(k_ref[0, h, rows, :], _q_t(qt_ref, h)) for h in range(g)]
        stacks = []
        for z in zs:
            sp = jnp.maximum(z, 0.0) + jnp.log(1.0 + jnp.exp(-jnp.abs(z)))
            if keep is not None:
                sp = jnp.where(keep, sp, 0.0)
            hi = sp.astype(MXU_DTYPE)
            lo = (sp - hi.astype(F32)).astype(MXU_DTYPE)
            stacks.append(jnp.concatenate([hi, lo], axis=0))
        csums = [_dot(suffix_t, st) for st in stacks]
        ws = []
        for h in range(g):
            a = jnp.exp(zs[h] + state[h][0] - csums[h])
            if keep is not None:
                a = jnp.where(keep, a, 0.0)
            ws.append(a.astype(MXU_DTYPE))
        return tuple((state[h][0] - csums[h][0:1, :],
                      state[h][1] + _dot(_v_t(vt_ref, h, kc), ws[h])) for h in range(g))

    def any_live(state):
        r_max = functools.reduce(jnp.maximum, [st[0] for st in state])
        return (jnp.max(r_max) > EXP_ZERO).astype(jnp.int32)

    state = ((jnp.zeros((1, tq), F32), jnp.zeros((HEAD_DIM, tq), F32)),) * g
    for d in reversed(range(n_diag)):
        state = chunks(qi * n_diag + d, state, d)

    def body(carry):
        i, _, st = carry
        st = chunks(qi * n_diag - 1 - i, st, None)
        return i + 1, any_live(st), st

    _, _, state = lax.while_loop(lambda c: (c[0] < qi * n_diag) & (c[1] > 0), body,
                                 (jnp.int32(0), any_live(state), state))
    _store_heads(o_ref, [st[1] for st in state])


def _fox_kernel(fend_ref, kn_ref, qt_ref, k_ref, vt_ref, o_ref):
    g, tq = qt_ref.shape[1], qt_ref.shape[2] * qt_ref.shape[4]
    tk = vt_ref.shape[4]
    n_diag = tq // tk
    n_chunks = k_ref.shape[2] // tk
    bi, gi, qi = pl.program_id(0), pl.program_id(1), pl.program_id(2)

    def chunks(kc, state):
        rows = pl.ds(pl.multiple_of(kc * tk, tk), tk)
        ss = [_dot(k_ref[0, h, rows, :], qts[h]) for h in range(g)]
        stats, ps = [], []
        for h in range(g):
            m, l, _ = state[h]
            m_new = jnp.maximum(m, jnp.max(ss[h], axis=0, keepdims=True))
            alpha = jnp.exp(m - m_new)
            p = jnp.exp(ss[h] - m_new)
            stats.append((m_new, alpha * l + jnp.sum(p, axis=0, keepdims=True), alpha))
            ps.append(p.astype(MXU_DTYPE))
        return tuple((stats[h][0], stats[h][1],
                      stats[h][2] * state[h][2] + _dot(_v_t(vt_ref, h, kc), ps[h]))
                     for h in range(g))

    def widen(x, lo, fill):
        return x if lo == 0 else jnp.concatenate([jnp.full((x.shape[0], lo), fill, F32), x], axis=1)

    def block_softmax(h):
        qt = qts[h]
        ss = []
        for d in range(n_diag):
            rows = pl.ds(pl.multiple_of((qi * n_diag + d) * tk, tk), tk)
            key, query = _key_query_iota(tk, tq - d * tk)
            ss.append(jnp.where(key <= query, _dot(k_ref[0, h, rows, :], qt[:, d * tk:]), NEG_BIG))
        m = functools.reduce(jnp.maximum, [widen(jnp.max(s, axis=0, keepdims=True), d * tk, NEG_BIG)
                                           for d, s in enumerate(ss)])
        l = jnp.zeros((1, tq), F32)
        acc = jnp.zeros((HEAD_DIM, tq), F32)
        for d, s in enumerate(ss):
            p = jnp.exp(s - m[:, d * tk:])
            l = l + widen(jnp.sum(p, axis=0, keepdims=True), d * tk, 0.0)
            acc = acc + widen(_dot(_v_t(vt_ref, h, qi * n_diag + d), p.astype(MXU_DTYPE)), d * tk, 0.0)
        return m, l, acc

    qts = [_q_t(qt_ref, h) for h in range(g)]
    state = tuple(block_softmax(h) for h in range(g))

    heads = [(bi * (N_HEADS_FOX) + gi * g + h) for h in range(g)]
    slack = []
    for h in range(g):
        qt = qts[h].astype(F32)
        qn = jnp.sqrt(jnp.sum(qt[0:HEAD_DIM] * qt[0:HEAD_DIM], axis=0, keepdims=True))
        f_t = jnp.sum(qt[HEAD_DIM:HEAD_DIM + 8], axis=0, keepdims=True) - 3.0
        bound = qn * (kn_ref[heads[h]] * 1.001) + f_t - state[h][0]
        slack.append(jnp.max(bound))

    def any_live(kc):
        kc = jnp.maximum(kc, 0)
        live = [slack[h] - fend_ref[heads[h] * n_chunks + kc] > FOX_SKIP for h in range(g)]
        return functools.reduce(jnp.logical_or, live).astype(jnp.int32)

    def body(carry):
        i, _, st = carry
        kc = qi * n_diag - 1 - i
        return i + 1, any_live(kc - 1), chunks(kc, st)

    _, _, state = lax.while_loop(lambda c: (c[0] < qi * n_diag) & (c[1] > 0), body,
                                 (jnp.int32(0), any_live(qi * n_diag - 1), state))
    _store_heads(o_ref, [st[2] / st[1] for st in state])


def _attention(body, name, qt_aug, k_aug, vt, head0, n_heads, g, q_tiles, prefetch=()):
    b, _, nq, _, tq = qt_aug.shape
    s = k_aug.shape[2]
    nk, tk = vt.shape[2], vt.shape[4]
    g0 = head0 // g
    return pl.pallas_call(
        body,
        grid_spec=pltpu.PrefetchScalarGridSpec(
            num_scalar_prefetch=len(prefetch),
            grid=(b, n_heads // g, nq // q_tiles),
            in_specs=[
                pl.BlockSpec((1, g, q_tiles, LANES, tq), lambda i, p, j, *_: (i, g0 + p, j, 0, 0)),
                pl.BlockSpec((1, g, s, LANES), lambda i, p, j, *_: (i, g0 + p, 0, 0)),
                pl.BlockSpec((1, g // 2, nk, LANES, tk), lambda i, p, j, *_: (i, g0 + p, 0, 0, 0)),
            ],
            out_specs=pl.BlockSpec((1, q_tiles * tq, g * HEAD_DIM), lambda i, p, j, *_: (i, j, p)),
        ),
        out_shape=jax.ShapeDtypeStruct((b, nq * tq, n_heads * HEAD_DIM), F32),
        compiler_params=_params(3),
        name=name,
    )(*prefetch, qt_aug, k_aug, vt)


def _outproj_kernel(osb_ref, ofx_ref, x_ref, gsb_ref, gfx_ref, wout_ref, gpost_ref, o_ref):
    d_sb = osb_ref.shape[1]
    nsb = _rms(osb_ref[...], gsb_ref[...]).astype(MXU_DTYPE)
    nfx = _rms(ofx_ref[...], gfx_ref[...]).astype(MXU_DTYPE)
    m = _dot(nsb, wout_ref[0:d_sb, :]) + _dot(nfx, wout_ref[d_sb:, :])
    o_ref[...] = x_ref[...] + _rms(m, gpost_ref[...])


def _out_proj(o_sb, o_fx, x2, g_sb, g_fx, w_out, g_post):
    n, d = x2.shape
    tm = min(PROJ_ROWS, n)
    row = lambda w: pl.BlockSpec((tm, w), lambda i: (i, 0))
    full = lambda a: pl.BlockSpec(a.shape, lambda i: (0, 0))
    return pl.pallas_call(
        _outproj_kernel,
        grid=(n // tm,),
        in_specs=[row(o_sb.shape[1]), row(o_fx.shape[1]), row(d), full(g_sb), full(g_fx),
                  full(w_out), full(g_post)],
        out_specs=row(d),
        out_shape=jax.ShapeDtypeStruct((n, d), F32),
        compiler_params=_params(1),
        name="out_proj",
    )(o_sb, o_fx, x2, g_sb, g_fx, w_out, g_post)


def _ple_kernel(x_ref, p_ref, gpre_ref, wgate_ref, wproj_ref, gpost_ref, o_ref):
    x = x_ref[...]
    hn = _rms(x, gpre_ref[...]).astype(MXU_DTYPE)
    gate = jax.nn.sigmoid(_dot(hn, wgate_ref[...]))
    e = gate * _dot(p_ref[...].astype(MXU_DTYPE), wproj_ref[...])
    o_ref[...] = x + _rms(e, gpost_ref[...])


def _ple(x2, p2, g_pre, w_gate, w_proj, g_post):
    n, d = x2.shape
    tm = min(PROJ_ROWS, n)
    row = lambda w: pl.BlockSpec((tm, w), lambda i: (i, 0))
    full = lambda a: pl.BlockSpec(a.shape, lambda i: (0, 0))
    return pl.pallas_call(
        _ple_kernel,
        grid=(n // tm,),
        in_specs=[row(d), row(p2.shape[1]), full(g_pre), full(w_gate), full(w_proj), full(g_post)],
        out_specs=row(d),
        out_shape=jax.ShapeDtypeStruct((n, d), F32),
        compiler_params=_params(1),
        name="ple",
    )(x2, p2, g_pre, w_gate, w_proj, g_post)


def kernel(x, p, ffn1_pre_g, ffn1_w_gate, ffn1_w_up, ffn1_w_down, ffn1_post_g, mix_pre_g, w_in, b_forget, sb_group_g, fox_group_g, w_out, mix_post_g, ffn2_pre_g, ffn2_w_gate, ffn2_w_up, ffn2_w_down, ffn2_post_g, ple_pre_g, w_ple_gate, w_ple_proj, ple_post_g):
    b, s, d = x.shape
    depth = p.shape[0]
    n = b * s
    d_qkv = 3 * (N_HEADS_SB + N_HEADS_FOX) * HEAD_DIM
    w = lambda a: a.astype(MXU_DTYPE)
    x2 = x.reshape(n, d)
    for i in range(depth):
        x2 = _ffn(x2, ffn1_pre_g[i:i + 1], w(ffn1_w_gate[i]), w(ffn1_w_up[i]), w(ffn1_w_down[i]),
                  ffn1_post_g[i:i + 1])
        wf = jnp.pad(w_in[i][:, d_qkv:], ((0, 0), (0, LANES - N_HEADS_FOX)))
        bf = jnp.pad(b_forget[i:i + 1], ((0, 0), (0, LANES - N_HEADS_FOX)))
        qt_aug, k_aug, vt, fend, kn2 = _in_proj(x2.reshape(b, s, d), mix_pre_g[i:i + 1],
                                                w(w_in[i][:, :d_qkv]), w(wf), bf)
        fend = fend.reshape(b, -1, LANES)[:, :, :N_HEADS_FOX].transpose(0, 2, 1).reshape(-1)
        kn = jnp.sqrt(jnp.max(kn2[:, :, 0, :N_HEADS_FOX], axis=1)).reshape(-1)
        o_sb = _attention(_sb_kernel, "attn_sb", qt_aug, k_aug, vt, 0, N_HEADS_SB, SB_HEADS, SB_QTILES)
        o_fx = _attention(_fox_kernel, "attn_fox", qt_aug, k_aug, vt, N_HEADS_SB, N_HEADS_FOX,
                          FOX_HEADS, FOX_QTILES, prefetch=(fend, kn))
        x2 = _out_proj(o_sb.reshape(n, -1), o_fx.reshape(n, -1), x2, sb_group_g[i:i + 1],
                       fox_group_g[i:i + 1], w(w_out[i]), mix_post_g[i:i + 1])
        x2 = _ffn(x2, ffn2_pre_g[i:i + 1], w(ffn2_w_gate[i]), w(ffn2_w_up[i]), w(ffn2_w_down[i]),
                  ffn2_post_g[i:i + 1])
        x2 = _ple(x2, p[i].reshape(n, -1), ple_pre_g[i:i + 1], w(w_ple_gate[i]), w(w_ple_proj[i]),
                  ple_post_g[i:i + 1])
    return x2.reshape(b, s, d)
```

```python
import functools

import jax
import jax.numpy as jnp
from jax import lax
from jax.experimental import pallas as pl
from jax.experimental.pallas import tpu as pltpu

HEAD_DIM = 64
N_HEADS_SB = 8
N_HEADS_FOX = 8
EPS = 1e-6
FFN_RES_WEIGHT = 0.5

LANES = 128
MXU_DTYPE = jnp.bfloat16
VMEM_LIMIT_BYTES = 56 * 1024 * 1024

FFN_ROWS = 1024
FFN_COLS = 1408
PROJ_ROWS = 512
ATTN_BLOCK = 256
KEY_CHUNK = 256
SB_HEADS, SB_QTILES = 8, 1
FOX_HEADS, FOX_QTILES = 2, 4
EXP_ZERO = -105.0
FOX_SKIP = EXP_ZERO - 1.0

F32 = jnp.float32
NEG_BIG = -1e30


def _dot(a, b):
    return jnp.dot(a, b, preferred_element_type=F32)


def _rms(x, g):
    return x * lax.rsqrt(jnp.mean(x * x, axis=-1, keepdims=True) + EPS) * g


def _split3(x):
    hi = x.astype(MXU_DTYPE).astype(F32)
    r = x - hi
    mid = r.astype(MXU_DTYPE).astype(F32)
    lo = (r - mid).astype(MXU_DTYPE).astype(F32)
    return hi, mid, lo


def _params(n_axes):
    return pltpu.CompilerParams(
        dimension_semantics=("arbitrary",) * n_axes,
        vmem_limit_bytes=VMEM_LIMIT_BYTES,
    )


def _ffn_kernel(x_ref, pre_g_ref, wg_ref, wu_ref, wd_ref, post_g_ref, o_ref, hn_ref, acc_ref):
    c = pl.program_id(1)

    @pl.when(c == 0)
    def _():
        hn_ref[...] = _rms(x_ref[...], pre_g_ref[...]).astype(MXU_DTYPE)
        acc_ref[...] = jnp.zeros_like(acc_ref)

    hn = hn_ref[...]
    g = _dot(hn, wg_ref[...])
    u = _dot(hn, wu_ref[...])
    act = (g * jax.nn.sigmoid(g)) * u
    acc_ref[...] += _dot(act.astype(MXU_DTYPE), wd_ref[...])

    @pl.when(c == pl.num_programs(1) - 1)
    def _():
        o_ref[...] = x_ref[...] + FFN_RES_WEIGHT * _rms(acc_ref[...], post_g_ref[...])


def _ffn(x2, pre_g, wg, wu, wd, post_g):
    n, d = x2.shape
    dff = wg.shape[1]
    tm = min(FFN_ROWS, n)
    tf = FFN_COLS
    return pl.pallas_call(
        _ffn_kernel,
        grid=(n // tm, dff // tf),
        in_specs=[
            pl.BlockSpec((tm, d), lambda i, c: (i, 0)),
            pl.BlockSpec((1, d), lambda i, c: (0, 0)),
            pl.BlockSpec((d, tf), lambda i, c: (0, c)),
            pl.BlockSpec((d, tf), lambda i, c: (0, c)),
            pl.BlockSpec((tf, d), lambda i, c: (c, 0)),
            pl.BlockSpec((1, d), lambda i, c: (0, 0)),
        ],
        out_specs=pl.BlockSpec((tm, d), lambda i, c: (i, 0)),
        out_shape=jax.ShapeDtypeStruct((n, d), F32),
        scratch_shapes=[pltpu.VMEM((tm, d), MXU_DTYPE), pltpu.VMEM((tm, d), F32)],
        compiler_params=_params(2),
        name="ffn",
    )(x2, pre_g, wg, wu, wd, post_g)


def _log_sigmoid(x):
    return jnp.minimum(x, 0.0) - jnp.log(1.0 + jnp.exp(-jnp.abs(x)))


def _inproj_kernel(x_ref, g_ref, wqkv_ref, wf_ref, bf_ref,
                   qt_ref, k_ref, vt_ref, fend_ref, kn2_ref, carry_ref, *, tq, tk):
    tm = x_ref.shape[1]
    d_sb = N_HEADS_SB * HEAD_DIM
    d_fx = N_HEADS_FOX * HEAD_DIM
    si = pl.program_id(1)

    @pl.when(si == 0)
    def _():
        carry_ref[...] = jnp.zeros_like(carry_ref)

    hn = _rms(x_ref[0], g_ref[...]).astype(MXU_DTYPE)

    ls = _log_sigmoid(_dot(hn, wf_ref[...]) + bf_ref[...])
    r = lax.broadcasted_iota(jnp.int32, (tm, tm), 0)
    c = lax.broadcasted_iota(jnp.int32, (tm, tm), 1)
    tri = (r >= c).astype(MXU_DTYPE)
    ls_hi, ls_mid, ls_lo = _split3(ls)
    fcum = (_dot(tri, ls_hi.astype(MXU_DTYPE)) + _dot(tri, ls_mid.astype(MXU_DTYPE))
            + _dot(tri, ls_lo.astype(MXU_DTYPE))) + carry_ref[...]
    carry_ref[...] = fcum[tm - 1:tm, :]
    for j in range(tm // tk):
        fend_ref[0, 0, j:j + 1, :] = fcum[(j + 1) * tk - 1:(j + 1) * tk, :]

    lane = lax.broadcasted_iota(jnp.int32, (tm, LANES), 1)
    lane1 = lax.broadcasted_iota(jnp.int32, (1, LANES), 1)
    low = lane < HEAD_DIM

    def head_halves(col0):
        pr = _dot(hn, wqkv_ref[:, col0:col0 + LANES])
        return pr, pltpu.roll(pr, HEAD_DIM, 1)

    def store_q(h, q_aug):
        qt = q_aug.T
        for j in range(tm // tq):
            qt_ref[0, h, j] = qt[:, j * tq:(j + 1) * tq].astype(MXU_DTYPE)

    scale = HEAD_DIM ** -0.5
    for grp in range(d_sb // LANES):
        for half, pr in enumerate(head_halves(grp * LANES)):
            store_q(2 * grp + half, jnp.where(low, pr * scale, 0.0))
        for half, pr in enumerate(head_halves(d_sb + grp * LANES)):
            k_ref[0, 2 * grp + half] = jnp.where(low, pr, 0.0).astype(MXU_DTYPE)

    q0 = 3 * d_sb
    kn2 = jnp.zeros((1, LANES), F32)
    for grp in range(d_fx // LANES):
        qs = head_halves(q0 + grp * LANES)
        ks = head_halves(q0 + d_fx + grp * LANES)
        for half in range(2):
            hf = 2 * grp + half
            f_hi, f_mid, f_lo = _split3(fcum[:, hf:hf + 1])
            q_bias = jnp.where(lane == HEAD_DIM, f_hi,
                     jnp.where(lane == HEAD_DIM + 1, f_mid,
                     jnp.where(lane == HEAD_DIM + 2, f_lo,
                     jnp.where(lane < HEAD_DIM + 6, 1.0, 0.0))))
            k_bias = jnp.where(lane < HEAD_DIM + 3, 1.0,
                     jnp.where(lane == HEAD_DIM + 3, -f_hi,
                     jnp.where(lane == HEAD_DIM + 4, -f_mid,
                     jnp.where(lane == HEAD_DIM + 5, -f_lo, 0.0))))
            store_q(N_HEADS_SB + hf, jnp.where(low, qs[half] * scale, q_bias))
            k_used = jnp.where(low, ks[half], 0.0).astype(MXU_DTYPE).astype(F32)
            norm2 = jnp.max(jnp.sum(k_used * k_used, axis=1, keepdims=True), axis=0, keepdims=True)
            kn2 = jnp.where(lane1 == hf, norm2, kn2)
            k_ref[0, N_HEADS_SB + hf] = jnp.where(low, ks[half], k_bias).astype(MXU_DTYPE)
    kn2_ref[0, 0] = kn2

    def store_v(p0, col0, width):
        for grp in range(width // LANES):
            vt = _dot(hn, wqkv_ref[:, col0 + grp * LANES:col0 + (grp + 1) * LANES]).T
            for j in range(tm // tk):
                vt_ref[0, p0 + grp, j] = vt[:, j * tk:(j + 1) * tk].astype(MXU_DTYPE)

    store_v(0, 2 * d_sb, d_sb)
    store_v(d_sb // LANES, q0 + 2 * d_fx, d_fx)


def _in_proj(x3, g, wqkv, wf, bf):
    b, s, d = x3.shape
    tm = min(PROJ_ROWS, s)
    tq = min(ATTN_BLOCK, s)
    tk = min(KEY_CHUNK, s)
    nh = N_HEADS_SB + N_HEADS_FOX
    return pl.pallas_call(
        functools.partial(_inproj_kernel, tq=tq, tk=tk),
        grid=(b, s // tm),
        in_specs=[
            pl.BlockSpec((1, tm, d), lambda i, j: (i, j, 0)),
            pl.BlockSpec((1, d), lambda i, j: (0, 0)),
            pl.BlockSpec(wqkv.shape, lambda i, j: (0, 0)),
            pl.BlockSpec(wf.shape, lambda i, j: (0, 0)),
            pl.BlockSpec((1, LANES), lambda i, j: (0, 0)),
        ],
        out_specs=[
            pl.BlockSpec((1, nh, tm // tq, LANES, tq), lambda i, j: (i, 0, j, 0, 0)),
            pl.BlockSpec((1, nh, tm, LANES), lambda i, j: (i, 0, j, 0)),
            pl.BlockSpec((1, nh // 2, tm // tk, LANES, tk), lambda i, j: (i, 0, j, 0, 0)),
            pl.BlockSpec((1, 1, tm // tk, LANES), lambda i, j: (i, j, 0, 0)),
            pl.BlockSpec((1, 1, 1, LANES), lambda i, j: (i, j, 0, 0)),
        ],
        out_shape=[
            jax.ShapeDtypeStruct((b, nh, s // tq, LANES, tq), MXU_DTYPE),
            jax.ShapeDtypeStruct((b, nh, s, LANES), MXU_DTYPE),
            jax.ShapeDtypeStruct((b, nh // 2, s // tk, LANES, tk), MXU_DTYPE),
            jax.ShapeDtypeStruct((b, s // tm, tm // tk, LANES), F32),
            jax.ShapeDtypeStruct((b, s // tm, 1, LANES), F32),
        ],
        scratch_shapes=[pltpu.VMEM((1, LANES), F32)],
        compiler_params=_params(2),
        name="in_proj",
    )(x3, g, wqkv, wf, bf)


def _key_query_iota(tk, tq):
    return (lax.broadcasted_iota(jnp.int32, (tk, tq), 0), lax.broadcasted_iota(jnp.int32, (tk, tq), 1))


def _q_t(qt_ref, h):
    tiles = [qt_ref[0, h, j] for j in range(qt_ref.shape[2])]
    return tiles[0] if len(tiles) == 1 else jnp.concatenate(tiles, axis=1)


def _store_heads(o_ref, acc_t):
    for p in range(len(acc_t) // 2):
        pair_t = jnp.concatenate([acc_t[2 * p], acc_t[2 * p + 1]], axis=0)
        o_ref[0, :, p * LANES:(p + 1) * LANES] = pair_t.T


def _v_t(vt_ref, h, kc):
    half = (h % 2) * HEAD_DIM
    return vt_ref[0, h // 2, kc, half:half + HEAD_DIM, :]


def _sb_kernel(qt_ref, k_ref, vt_ref, o_ref):
    g, tq = qt_ref.shape[1], qt_ref.shape[2] * qt_ref.shape[4]
    tk = vt_ref.shape[4]
    n_diag = tq // tk
    qi = pl.program_id(2)
    key, query = _key_query_iota(tk, tq)
    ss = lax.broadcasted_iota(jnp.int32, (tk, 2 * tk), 0)
    jj = lax.broadcasted_iota(jnp.int32, (tk, 2 * tk), 1)
    suffix_t = (jnp.where(jj < tk, jj, jj - tk) >= ss).astype(MXU_DTYPE)

    def chunks(kc, state, diag):
        rows = pl.ds(pl.multiple_of(kc * tk, tk), tk)
        keep = None if diag is None else (key + diag * tk) < query
        zs = [_dot(k_ref[0, h, rows, :], _q_t(qt_ref, h)) for h in range(g)]
        stacks = []
        for z in zs:
            sp = jnp.maximum(z, 0.0) + jnp.log(1.0 + jnp.exp(-jnp.abs(z)))
            if keep is not None:
                sp = jnp.where(keep, sp, 0.0)
            hi = sp.astype(MXU_DTYPE)
            lo = (sp - hi.astype(F32)).astype(MXU_DTYPE)
            stacks.append(jnp.concatenate([hi, lo], axis=0))
        csums = [_dot(suffix_t, st) for st in stacks]
        ws = []
        for h in range(g):
            a = jnp.exp(zs[h] + state[h][0] - csums[h])
            if keep is not None:
                a = jnp.where(keep, a, 0.0)
            ws.append(a.astype(MXU_DTYPE))
        return tuple((state[h][0] - csums[h][0:1, :],
                      state[h][1] + _dot(_v_t(vt_ref, h, kc), ws[h])) for h in range(g))

    def any_live(state):
        r_max = functools.reduce(jnp.maximum, [st[0] for st in state])
        return (jnp.max(r_max) > EXP_ZERO).astype(jnp.int32)

    state = ((jnp.zeros((1, tq), F32), jnp.zeros((HEAD_DIM, tq), F32)),) * g
    for d in reversed(range(n_diag)):
        state = chunks(qi * n_diag + d, state, d)

    def body(carry):
        i, _, st = carry
        st = chunks(qi * n_diag - 1 - i, st, None)
        return i + 1, any_live(st), st

    _, _, state = lax.while_loop(lambda c: (c[0] < qi * n_diag) & (c[1] > 0), body,
                                 (jnp.int32(0), any_live(state), state))
    _store_heads(o_ref, [st[1] for st in state])


def _fox_kernel(fend_ref, kn_ref, qt_ref, k_ref, vt_ref, o_ref):
    g, tq = qt_ref.shape[1], qt_ref.shape[2] * qt_ref.shape[4]
    tk = vt_ref.shape[4]
    n_diag = tq // tk
    n_chunks = k_ref.shape[2] // tk
    bi, gi, qi = pl.program_id(0), pl.program_id(1), pl.program_id(2)

    def chunks(kc, state):
        rows = pl.ds(pl.multiple_of(kc * tk, tk), tk)
        ss = [_dot(k_ref[0, h, rows, :], qts[h]) for h in range(g)]
        stats, ps = [], []
        for h in range(g):
            m, l, _ = state[h]
            m_new = jnp.maximum(m, jnp.max(ss[h], axis=0, keepdims=True))
            alpha = jnp.exp(m - m_new)
            p = jnp.exp(ss[h] - m_new)
            stats.append((m_new, alpha * l + jnp.sum(p, axis=0, keepdims=True), alpha))
            ps.append(p.astype(MXU_DTYPE))
        return tuple((stats[h][0], stats[h][1],
                      stats[h][2] * state[h][2] + _dot(_v_t(vt_ref, h, kc), ps[h]))
                     for h in range(g))

    def widen(x, lo, fill):
        return x if lo == 0 else jnp.concatenate([jnp.full((x.shape[0], lo), fill, F32), x], axis=1)

    def block_softmax(h):
        qt = qts[h]
        ss = []
        for d in range(n_diag):
            rows = pl.ds(pl.multiple_of((qi * n_diag + d) * tk, tk), tk)
            key, query = _key_query_iota(tk, tq - d * tk)
            ss.append(jnp.where(key <= query, _dot(k_ref[0, h, rows, :], qt[:, d * tk:]), NEG_BIG))
        m = functools.reduce(jnp.maximum, [widen(jnp.max(s, axis=0, keepdims=True), d * tk, NEG_BIG)
                                           for d, s in enumerate(ss)])
        l = jnp.zeros((1, tq), F32)
        acc = jnp.zeros((HEAD_DIM, tq), F32)
        for d, s in enumerate(ss):
            p = jnp.exp(s - m[:, d * tk:])
            l = l + widen(jnp.sum(p, axis=0, keepdims=True), d * tk, 0.0)
            acc = acc + widen(_dot(_v_t(vt_ref, h, qi * n_diag + d), p.astype(MXU_DTYPE)), d * tk, 0.0)
        return m, l, acc

    qts = [_q_t(qt_ref, h) for h in range(g)]
    state = tuple(block_softmax(h) for h in range(g))

    heads = [(bi * (N_HEADS_FOX) + gi * g + h) for h in range(g)]
    slack = []
    for h in range(g):
        qt = qts[h].astype(F32)
        qn = jnp.sqrt(jnp.sum(qt[0:HEAD_DIM] * qt[0:HEAD_DIM], axis=0, keepdims=True))
        f_t = jnp.sum(qt[HEAD_DIM:HEAD_DIM + 8], axis=0, keepdims=True) - 3.0
        bound = qn * (kn_ref[heads[h]] * 1.001) + f_t - state[h][0]
        slack.append(jnp.max(bound))

    def any_live(kc):
        kc = jnp.maximum(kc, 0)
        live = [slack[h] - fend_ref[heads[h] * n_chunks + kc] > FOX_SKIP for h in range(g)]
        return functools.reduce(jnp.logical_or, live).astype(jnp.int32)

    def body(carry):
        i, _, st = carry
        kc = qi * n_diag - 1 - i
        return i + 1, any_live(kc - 1), chunks(kc, st)

    _, _, state = lax.while_loop(lambda c: (c[0] < qi * n_diag) & (c[1] > 0), body,
                                 (jnp.int32(0), any_live(qi * n_diag - 1), state))
    _store_heads(o_ref, [st[2] / st[1] for st in state])


def _attention(body, name, qt_aug, k_aug, vt, head0, n_heads, g, q_tiles, prefetch=()):
    b, _, nq, _, tq = qt_aug.shape
    s = k_aug.shape[2]
    nk, tk = vt.shape[2], vt.shape[4]
    g0 = head0 // g
    return pl.pallas_call(
        body,
        grid_spec=pltpu.PrefetchScalarGridSpec(
            num_scalar_prefetch=len(prefetch),
            grid=(b, n_heads // g, nq // q_tiles),
            in_specs=[
                pl.BlockSpec((1, g, q_tiles, LANES, tq), lambda i, p, j, *_: (i, g0 + p, j, 0, 0)),
                pl.BlockSpec((1, g, s, LANES), lambda i, p, j, *_: (i, g0 + p, 0, 0),
                             pipeline_mode=pl.Buffered(1)),
                pl.BlockSpec((1, g // 2, nk, LANES, tk), lambda i, p, j, *_: (i, g0 + p, 0, 0, 0),
                             pipeline_mode=pl.Buffered(1)),
            ],
            out_specs=pl.BlockSpec((1, q_tiles * tq, g * HEAD_DIM), lambda i, p, j, *_: (i, j, p)),
        ),
        out_shape=jax.ShapeDtypeStruct((b, nq * tq, n_heads * HEAD_DIM), F32),
        compiler_params=_params(3),
        name=name,
    )(*prefetch, qt_aug, k_aug, vt)


def _outproj_kernel(osb_ref, ofx_ref, x_ref, gsb_ref, gfx_ref, wout_ref, gpost_ref, o_ref):
    d_sb = osb_ref.shape[1]
    nsb = _rms(osb_ref[...], gsb_ref[...]).astype(MXU_DTYPE)
    nfx = _rms(ofx_ref[...], gfx_ref[...]).astype(MXU_DTYPE)
    m = _dot(nsb, wout_ref[0:d_sb, :]) + _dot(nfx, wout_ref[d_sb:, :])
    o_ref[...] = x_ref[...] + _rms(m, gpost_ref[...])


def _out_proj(o_sb, o_fx, x2, g_sb, g_fx, w_out, g_post):
    n, d = x2.shape
    tm = min(PROJ_ROWS, n)
    row = lambda w: pl.BlockSpec((tm, w), lambda i: (i, 0))
    full = lambda a: pl.BlockSpec(a.shape, lambda i: (0, 0))
    return pl.pallas_call(
        _outproj_kernel,
        grid=(n // tm,),
        in_specs=[row(o_sb.shape[1]), row(o_fx.shape[1]), row(d), full(g_sb), full(g_fx),
                  full(w_out), full(g_post)],
        out_specs=row(d),
        out_shape=jax.ShapeDtypeStruct((n, d), F32),
        compiler_params=_params(1),
        name="out_proj",
    )(o_sb, o_fx, x2, g_sb, g_fx, w_out, g_post)


def _ple_kernel(x_ref, p_ref, gpre_ref, wgate_ref, wproj_ref, gpost_ref, o_ref):
    x = x_ref[...]
    hn = _rms(x, gpre_ref[...]).astype(MXU_DTYPE)
    gate = jax.nn.sigmoid(_dot(hn, wgate_ref[...]))
    e = gate * _dot(p_ref[...].astype(MXU_DTYPE), wproj_ref[...])
    o_ref[...] = x + _rms(e, gpost_ref[...])


def _ple(x2, p2, g_pre, w_gate, w_proj, g_post):
    n, d = x2.shape
    tm = min(PROJ_ROWS, n)
    row = lambda w: pl.BlockSpec((tm, w), lambda i: (i, 0))
    full = lambda a: pl.BlockSpec(a.shape, lambda i: (0, 0))
    return pl.pallas_call(
        _ple_kernel,
        grid=(n // tm,),
        in_specs=[row(d), row(p2.shape[1]), full(g_pre), full(w_gate), full(w_proj), full(g_post)],
        out_specs=row(d),
        out_shape=jax.ShapeDtypeStruct((n, d), F32),
        compiler_params=_params(1),
        name="ple",
    )(x2, p2, g_pre, w_gate, w_proj, g_post)


def kernel(x, p, ffn1_pre_g, ffn1_w_gate, ffn1_w_up, ffn1_w_down, ffn1_post_g, mix_pre_g, w_in, b_forget, sb_group_g, fox_group_g, w_out, mix_post_g, ffn2_pre_g, ffn2_w_gate, ffn2_w_up, ffn2_w_down, ffn2_post_g, ple_pre_g, w_ple_gate, w_ple_proj, ple_post_g):
    b, s, d = x.shape
    depth = p.shape[0]
    n = b * s
    d_qkv = 3 * (N_HEADS_SB + N_HEADS_FOX) * HEAD_DIM
    w = lambda a: a.astype(MXU_DTYPE)
    x2 = x.reshape(n, d)
    for i in range(depth):
        x2 = _ffn(x2, ffn1_pre_g[i:i + 1], w(ffn1_w_gate[i]), w(ffn1_w_up[i]), w(ffn1_w_down[i]),
                  ffn1_post_g[i:i + 1])
        wf = jnp.pad(w_in[i][:, d_qkv:], ((0, 0), (0, LANES - N_HEADS_FOX)))
        bf = jnp.pad(b_forget[i:i + 1], ((0, 0), (0, LANES - N_HEADS_FOX)))
        qt_aug, k_aug, vt, fend, kn2 = _in_proj(x2.reshape(b, s, d), mix_pre_g[i:i + 1],
                                                w(w_in[i][:, :d_qkv]), w(wf), bf)
        fend = fend.reshape(b, -1, LANES)[:, :, :N_HEADS_FOX].transpose(0, 2, 1).reshape(-1)
        kn = jnp.sqrt(jnp.max(kn2[:, :, 0, :N_HEADS_FOX], axis=1)).reshape(-1)
        o_sb = _attention(_sb_kernel, "attn_sb", qt_aug, k_aug, vt, 0, N_HEADS_SB, SB_HEADS, SB_QTILES)
        o_fx = _attention(_fox_kernel, "attn_fox", qt_aug, k_aug, vt, N_HEADS_SB, N_HEADS_FOX,
                          FOX_HEADS, FOX_QTILES, prefetch=(fend, kn))
        x2 = _out_proj(o_sb.reshape(n, -1), o_fx.reshape(n, -1), x2, sb_group_g[i:i + 1],
                       fox_group_g[i:i + 1], w(w_out[i]), mix_post_g[i:i + 1])
        x2 = _ffn(x2, ffn2_pre_g[i:i + 1], w(ffn2_w_gate[i]), w(ffn2_w_up[i]), w(ffn2_w_down[i]),
                  ffn2_post_g[i:i + 1])
        x2 = _ple(x2, p[i].reshape(n, -1), ple_pre_g[i:i + 1], w(w_ple_gate[i]), w(w_ple_proj[i]),
                  ple_post_g[i:i + 1])
    return x2.reshape(b, s, d)
```

```python
import functools

import jax
import jax.numpy as jnp
from jax import lax
from jax.experimental import pallas as pl
from jax.experimental.pallas import tpu as pltpu

HEAD_DIM = 64
N_HEADS_SB = 8
N_HEADS_FOX = 8
EPS = 1e-6
FFN_RES_WEIGHT = 0.5

LANES = 128
MXU_DTYPE = jnp.bfloat16
VMEM_LIMIT_BYTES = 56 * 1024 * 1024

FFN_ROWS = 1024
FFN_COLS = 1408
PROJ_ROWS = 512
ATTN_BLOCK = 256
KEY_CHUNK = 256
SB_HEADS, SB_QTILES = 8, 1
FOX_HEADS, FOX_QTILES = 2, 4
EXP_ZERO = -105.0
LOG2E = 1.4426950408889634
FOX_SKIP = -154.0

F32 = jnp.float32
NEG_BIG = -1e30


def _dot(a, b):
    return jnp.dot(a, b, preferred_element_type=F32)


def _rms(x, g):
    return x * lax.rsqrt(jnp.mean(x * x, axis=-1, keepdims=True) + EPS) * g


def _split3(x):
    hi = x.astype(MXU_DTYPE).astype(F32)
    r = x - hi
    mid = r.astype(MXU_DTYPE).astype(F32)
    lo = (r - mid).astype(MXU_DTYPE).astype(F32)
    return hi, mid, lo


def _params(n_axes):
    return pltpu.CompilerParams(
        dimension_semantics=("arbitrary",) * n_axes,
        vmem_limit_bytes=VMEM_LIMIT_BYTES,
    )


def _ffn_kernel(x_ref, pre_g_ref, wg_ref, wu_ref, wd_ref, post_g_ref, o_ref, hn_ref, acc_ref):
    c = pl.program_id(1)

    @pl.when(c == 0)
    def _():
        hn_ref[...] = _rms(x_ref[...], pre_g_ref[...]).astype(MXU_DTYPE)
        acc_ref[...] = jnp.zeros_like(acc_ref)

    hn = hn_ref[...]
    g = _dot(hn, wg_ref[...])
    u = _dot(hn, wu_ref[...])
    act = (g * jax.nn.sigmoid(g)) * u
    acc_ref[...] += _dot(act.astype(MXU_DTYPE), wd_ref[...])

    @pl.when(c == pl.num_programs(1) - 1)
    def _():
        o_ref[...] = x_ref[...] + FFN_RES_WEIGHT * _rms(acc_ref[...], post_g_ref[...])


def _ffn(x2, pre_g, wg, wu, wd, post_g):
    n, d = x2.shape
    dff = wg.shape[1]
    tm = min(FFN_ROWS, n)
    tf = FFN_COLS
    return pl.pallas_call(
        _ffn_kernel,
        grid=(n // tm, dff // tf),
        in_specs=[
            pl.BlockSpec((tm, d), lambda i, c: (i, 0)),
            pl.BlockSpec((1, d), lambda i, c: (0, 0)),
            pl.BlockSpec((d, tf), lambda i, c: (0, c)),
            pl.BlockSpec((d, tf), lambda i, c: (0, c)),
            pl.BlockSpec((tf, d), lambda i, c: (c, 0)),
            pl.BlockSpec((1, d), lambda i, c: (0, 0)),
        ],
        out_specs=pl.BlockSpec((tm, d), lambda i, c: (i, 0)),
        out_shape=jax.ShapeDtypeStruct((n, d), F32),
        scratch_shapes=[pltpu.VMEM((tm, d), MXU_DTYPE), pltpu.VMEM((tm, d), F32)],
        compiler_params=_params(2),
        name="ffn",
    )(x2, pre_g, wg, wu, wd, post_g)


def _log_sigmoid(x):
    return jnp.minimum(x, 0.0) - jnp.log(1.0 + jnp.exp(-jnp.abs(x)))


def _inproj_kernel(x_ref, g_ref, wqkv_ref, wf_ref, bf_ref,
                   qt_ref, k_ref, vt_ref, fend_ref, kn2_ref, carry_ref, *, tq, tk):
    tm = x_ref.shape[1]
    d_sb = N_HEADS_SB * HEAD_DIM
    d_fx = N_HEADS_FOX * HEAD_DIM
    si = pl.program_id(1)

    @pl.when(si == 0)
    def _():
        carry_ref[...] = jnp.zeros_like(carry_ref)

    hn = _rms(x_ref[0], g_ref[...]).astype(MXU_DTYPE)

    ls = _log_sigmoid(_dot(hn, wf_ref[...]) + bf_ref[...])
    r = lax.broadcasted_iota(jnp.int32, (tm, tm), 0)
    c = lax.broadcasted_iota(jnp.int32, (tm, tm), 1)
    tri = (r >= c).astype(MXU_DTYPE)
    ls_hi, ls_mid, ls_lo = _split3(ls)
    fcum = (_dot(tri, ls_hi.astype(MXU_DTYPE)) + _dot(tri, ls_mid.astype(MXU_DTYPE))
            + _dot(tri, ls_lo.astype(MXU_DTYPE))) + carry_ref[...]
    carry_ref[...] = fcum[tm - 1:tm, :]
    fcum = fcum * LOG2E
    for j in range(tm // tk):
        fend_ref[0, 0, j:j + 1, :] = fcum[(j + 1) * tk - 1:(j + 1) * tk, :]

    lane = lax.broadcasted_iota(jnp.int32, (tm, LANES), 1)
    lane1 = lax.broadcasted_iota(jnp.int32, (1, LANES), 1)
    low = lane < HEAD_DIM

    def head_halves(col0):
        pr = _dot(hn, wqkv_ref[:, col0:col0 + LANES])
        return pr, pltpu.roll(pr, HEAD_DIM, 1)

    def store_q(h, q_aug):
        qt = q_aug.T
        for j in range(tm // tq):
            qt_ref[0, h, j] = qt[:, j * tq:(j + 1) * tq].astype(MXU_DTYPE)

    scale = HEAD_DIM ** -0.5
    for grp in range(d_sb // LANES):
        for half, pr in enumerate(head_halves(grp * LANES)):
            store_q(2 * grp + half, jnp.where(low, pr * scale, 0.0))
        for half, pr in enumerate(head_halves(d_sb + grp * LANES)):
            k_ref[0, 2 * grp + half] = jnp.where(low, pr, 0.0).astype(MXU_DTYPE)

    q0 = 3 * d_sb
    kn2 = jnp.zeros((1, LANES), F32)
    for grp in range(d_fx // LANES):
        qs = head_halves(q0 + grp * LANES)
        ks = head_halves(q0 + d_fx + grp * LANES)
        for half in range(2):
            hf = 2 * grp + half
            f_hi, f_mid, f_lo = _split3(fcum[:, hf:hf + 1])
            q_bias = jnp.where(lane == HEAD_DIM, f_hi,
                     jnp.where(lane == HEAD_DIM + 1, f_mid,
                     jnp.where(lane == HEAD_DIM + 2, f_lo,
                     jnp.where(lane < HEAD_DIM + 6, 1.0, 0.0))))
            k_bias = jnp.where(lane < HEAD_DIM + 3, 1.0,
                     jnp.where(lane == HEAD_DIM + 3, -f_hi,
                     jnp.where(lane == HEAD_DIM + 4, -f_mid,
                     jnp.where(lane == HEAD_DIM + 5, -f_lo, 0.0))))
            store_q(N_HEADS_SB + hf, jnp.where(low, qs[half] * (scale * LOG2E), q_bias))
            k_used = jnp.where(low, ks[half], 0.0).astype(MXU_DTYPE).astype(F32)
            norm2 = jnp.max(jnp.sum(k_used * k_used, axis=1, keepdims=True), axis=0, keepdims=True)
            kn2 = jnp.where(lane1 == hf, norm2, kn2)
            k_ref[0, N_HEADS_SB + hf] = jnp.where(low, ks[half], k_bias).astype(MXU_DTYPE)
    kn2_ref[0, 0] = kn2

    def store_v(p0, col0, width):
        for grp in range(width // LANES):
            vt = _dot(hn, wqkv_ref[:, col0 + grp * LANES:col0 + (grp + 1) * LANES]).T
            for j in range(tm // tk):
                vt_ref[0, p0 + grp, j] = vt[:, j * tk:(j + 1) * tk].astype(MXU_DTYPE)

    store_v(0, 2 * d_sb, d_sb)
    store_v(d_sb // LANES, q0 + 2 * d_fx, d_fx)


def _in_proj(x3, g, wqkv, wf, bf):
    b, s, d = x3.shape
    tm = min(PROJ_ROWS, s)
    tq = min(ATTN_BLOCK, s)
    tk = min(KEY_CHUNK, s)
    nh = N_HEADS_SB + N_HEADS_FOX
    return pl.pallas_call(
        functools.partial(_inproj_kernel, tq=tq, tk=tk),
        grid=(b, s // tm),
        in_specs=[
            pl.BlockSpec((1, tm, d), lambda i, j: (i, j, 0)),
            pl.BlockSpec((1, d), lambda i, j: (0, 0)),
            pl.BlockSpec(wqkv.shape, lambda i, j: (0, 0)),
            pl.BlockSpec(wf.shape, lambda i, j: (0, 0)),
            pl.BlockSpec((1, LANES), lambda i, j: (0, 0)),
        ],
        out_specs=[
            pl.BlockSpec((1, nh, tm // tq, LANES, tq), lambda i, j: (i, 0, j, 0, 0)),
            pl.BlockSpec((1, nh, tm, LANES), lambda i, j: (i, 0, j, 0)),
            pl.BlockSpec((1, nh // 2, tm // tk, LANES, tk), lambda i, j: (i, 0, j, 0, 0)),
            pl.BlockSpec((1, 1, tm // tk, LANES), lambda i, j: (i, j, 0, 0)),
            pl.BlockSpec((1, 1, 1, LANES), lambda i, j: (i, j, 0, 0)),
        ],
        out_shape=[
            jax.ShapeDtypeStruct((b, nh, s // tq, LANES, tq), MXU_DTYPE),
            jax.ShapeDtypeStruct((b, nh, s, LANES), MXU_DTYPE),
            jax.ShapeDtypeStruct((b, nh // 2, s // tk, LANES, tk), MXU_DTYPE),
            jax.ShapeDtypeStruct((b, s // tm, tm // tk, LANES), F32),
            jax.ShapeDtypeStruct((b, s // tm, 1, LANES), F32),
        ],
        scratch_shapes=[pltpu.VMEM((1, LANES), F32)],
        compiler_params=_params(2),
        name="in_proj",
    )(x3, g, wqkv, wf, bf)


def _key_query_iota(tk, tq):
    return (lax.broadcasted_iota(jnp.int32, (tk, tq), 0), lax.broadcasted_iota(jnp.int32, (tk, tq), 1))


def _q_t(qt_ref, h):
    tiles = [qt_ref[0, h, j] for j in range(qt_ref.shape[2])]
    return tiles[0] if len(tiles) == 1 else jnp.concatenate(tiles, axis=1)


def _store_heads(o_ref, acc_t):
    for p in range(len(acc_t) // 2):
        pair_t = jnp.concatenate([acc_t[2 * p], acc_t[2 * p + 1]], axis=0)
        o_ref[0, :, p * LANES:(p + 1) * LANES] = pair_t.T


def _v_t(vt_ref, h, kc):
    half = (h % 2) * HEAD_DIM
    return vt_ref[0, h // 2, kc, half:half + HEAD_DIM, :]


def _sb_kernel(qt_ref, k_ref, vt_ref, o_ref):
    g, tq = qt_ref.shape[1], qt_ref.shape[2] * qt_ref.shape[4]
    tk = vt_ref.shape[4]
    n_diag = tq // tk
    qi = pl.program_id(2)
    key, query = _key_query_iota(tk, tq)
    ss = lax.broadcasted_iota(jnp.int32, (tk, 2 * tk), 0)
    jj = lax.broadcasted_iota(jnp.int32, (tk, 2 * tk), 1)
    suffix_t = (jnp.where(jj < tk, jj, jj - tk) >= ss).astype(MXU_DTYPE)

    def sweep(segments, state):
        zs, keeps, csums = [], [], []
        for kc, diag, _ in segments:
            rows = pl.ds(pl.multiple_of(kc * tk, tk), tk)
            keeps.append(None if diag is None else (key + diag * tk) < query)
            zs.append([_dot(k_ref[0, h, rows, :], qts[h]) for h in range(g)])
        for z_seg, keep in zip(zs, keeps):
            stacks = []
            for z in z_seg:
                sp = jnp.maximum(z, 0.0) + jnp.log(1.0 + jnp.exp(-jnp.abs(z)))
                if keep is not None:
                    sp = jnp.where(keep, sp, 0.0)
                hi = sp.astype(MXU_DTYPE)
                lo = (sp - hi.astype(F32)).astype(MXU_DTYPE)
                stacks.append(jnp.concatenate([hi, lo], axis=0))
            csums.append([_dot(suffix_t, st) for st in stacks])
        carry_r = [st[0] for st in state]
        ws = []
        for (kc, _, on), z_seg, keep, c_seg in zip(segments, zs, keeps, csums):
            w_seg = []
            for h in range(g):
                a = jnp.exp(z_seg[h] + carry_r[h] - c_seg[h])
                if keep is not None:
                    a = jnp.where(keep, a, 0.0)
                mass = c_seg[h][0:1, :]
                if on is not None:
                    a, mass = a * on, mass * on
                w_seg.append(a.astype(MXU_DTYPE))
                carry_r[h] = carry_r[h] - mass
            ws.append(w_seg)
        accs = [st[1] for st in state]
        for (kc, _, _), w_seg in zip(segments, ws):
            accs = [accs[h] + _dot(_v_t(vt_ref, h, kc), w_seg[h]) for h in range(g)]
        return tuple(zip(carry_r, accs))

    def any_live(state):
        r_max = functools.reduce(jnp.maximum, [st[0] for st in state])
        return (jnp.max(r_max) > EXP_ZERO).astype(jnp.int32)

    qts = [_q_t(qt_ref, h) for h in range(g)]
    n_left = qi * n_diag
    has_left = (n_left > 0).astype(F32)
    first = [(n_left + d, d, None) for d in reversed(range(n_diag))]
    first.append((jnp.maximum(n_left - 1, 0), None, has_left))
    state = sweep(first, ((jnp.zeros((1, tq), F32), jnp.zeros((HEAD_DIM, tq), F32)),) * g)

    def body(carry):
        i, _, st = carry
        st = sweep([(n_left - 2 - i, None, None)], st)
        return i + 1, any_live(st), st

    _, _, state = lax.while_loop(lambda c: (c[0] < n_left - 1) & (c[1] > 0), body,
                                 (jnp.int32(0), any_live(state), state))
    _store_heads(o_ref, [st[1] for st in state])


def _fox_kernel(fend_ref, kn_ref, qt_ref, k_ref, vt_ref, o_ref):
    g, tq = qt_ref.shape[1], qt_ref.shape[2] * qt_ref.shape[4]
    tk = vt_ref.shape[4]
    n_diag = tq // tk
    n_chunks = k_ref.shape[2] // tk
    bi, gi, qi = pl.program_id(0), pl.program_id(1), pl.program_id(2)

    def chunks(kc, state):
        rows = pl.ds(pl.multiple_of(kc * tk, tk), tk)
        ss = [_dot(k_ref[0, h, rows, :], qts[h]) for h in range(g)]
        stats, ps = [], []
        for h in range(g):
            m, l, _ = state[h]
            m_new = jnp.maximum(m, jnp.max(ss[h], axis=0, keepdims=True))
            alpha = jnp.exp2(m - m_new)
            p = jnp.exp2(ss[h] - m_new)
            stats.append((m_new, alpha * l + jnp.sum(p, axis=0, keepdims=True), alpha))
            ps.append(p.astype(MXU_DTYPE))
        return tuple((stats[h][0], stats[h][1],
                      stats[h][2] * state[h][2] + _dot(_v_t(vt_ref, h, kc), ps[h]))
                     for h in range(g))

    def widen(x, lo, fill):
        return x if lo == 0 else jnp.concatenate([jnp.full((x.shape[0], lo), fill, F32), x], axis=1)

    def block_softmax(h):
        qt = qts[h]
        ss = []
        for d in range(n_diag):
            rows = pl.ds(pl.multiple_of((qi * n_diag + d) * tk, tk), tk)
            key, query = _key_query_iota(tk, tq - d * tk)
            ss.append(jnp.where(key <= query, _dot(k_ref[0, h, rows, :], qt[:, d * tk:]), NEG_BIG))
        m = functools.reduce(jnp.maximum, [widen(jnp.max(s, axis=0, keepdims=True), d * tk, NEG_BIG)
                                           for d, s in enumerate(ss)])
        l = jnp.zeros((1, tq), F32)
        acc = jnp.zeros((HEAD_DIM, tq), F32)
        for d, s in enumerate(ss):
            p = jnp.exp2(s - m[:, d * tk:])
            l = l + widen(jnp.sum(p, axis=0, keepdims=True), d * tk, 0.0)
            acc = acc + widen(_dot(_v_t(vt_ref, h, qi * n_diag + d), p.astype(MXU_DTYPE)), d * tk, 0.0)
        return m, l, acc

    qts = [_q_t(qt_ref, h) for h in range(g)]
    state = tuple(block_softmax(h) for h in range(g))

    heads = [(bi * (N_HEADS_FOX) + gi * g + h) for h in range(g)]
    slack = []
    for h in range(g):
        qt = qts[h].astype(F32)
        qn = jnp.sqrt(jnp.sum(qt[0:HEAD_DIM] * qt[0:HEAD_DIM], axis=0, keepdims=True))
        f_t = jnp.sum(qt[HEAD_DIM:HEAD_DIM + 8], axis=0, keepdims=True) - 3.0
        bound = qn * (kn_ref[heads[h]] * 1.001) + f_t - state[h][0]
        slack.append(jnp.max(bound))

    def any_live(kc):
        kc = jnp.maximum(kc, 0)
        live = [slack[h] - fend_ref[heads[h] * n_chunks + kc] > FOX_SKIP for h in range(g)]
        return functools.reduce(jnp.logical_or, live).astype(jnp.int32)

    def body(carry):
        i, _, st = carry
        kc = qi * n_diag - 1 - i
        return i + 1, any_live(kc - 1), chunks(kc, st)

    _, _, state = lax.while_loop(lambda c: (c[0] < qi * n_diag) & (c[1] > 0), body,
                                 (jnp.int32(0), any_live(qi * n_diag - 1), state))
    _store_heads(o_ref, [st[2] / st[1] for st in state])


def _attention(body, name, qt_aug, k_aug, vt, head0, n_heads, g, q_tiles, prefetch=()):
    b, _, nq, _, tq = qt_aug.shape
    s = k_aug.shape[2]
    nk, tk = vt.shape[2], vt.shape[4]
    g0 = head0 // g
    resident = dict(pipeline_mode=pl.Buffered(1)) if g == n_heads else {}
    return pl.pallas_call(
        body,
        grid_spec=pltpu.PrefetchScalarGridSpec(
            num_scalar_prefetch=len(prefetch),
            grid=(b, n_heads // g, nq // q_tiles),
            in_specs=[
                pl.BlockSpec((1, g, q_tiles, LANES, tq), lambda i, p, j, *_: (i, g0 + p, j, 0, 0)),
                pl.BlockSpec((1, g, s, LANES), lambda i, p, j, *_: (i, g0 + p, 0, 0), **resident),
                pl.BlockSpec((1, g // 2, nk, LANES, tk), lambda i, p, j, *_: (i, g0 + p, 0, 0, 0),
                             **resident),
            ],
            out_specs=pl.BlockSpec((1, q_tiles * tq, g * HEAD_DIM), lambda i, p, j, *_: (i, j, p)),
        ),
        out_shape=jax.ShapeDtypeStruct((b, nq * tq, n_heads * HEAD_DIM), F32),
        compiler_params=_params(3),
        name=name,
    )(*prefetch, qt_aug, k_aug, vt)


def _outproj_kernel(osb_ref, ofx_ref, x_ref, gsb_ref, gfx_ref, wout_ref, gpost_ref, o_ref):
    d_sb = osb_ref.shape[1]
    nsb = _rms(osb_ref[...], gsb_ref[...]).astype(MXU_DTYPE)
    nfx = _rms(ofx_ref[...], gfx_ref[...]).astype(MXU_DTYPE)
    m = _dot(nsb, wout_ref[0:d_sb, :]) + _dot(nfx, wout_ref[d_sb:, :])
    o_ref[...] = x_ref[...] + _rms(m, gpost_ref[...])


def _out_proj(o_sb, o_fx, x2, g_sb, g_fx, w_out, g_post):
    n, d = x2.shape
    tm = min(PROJ_ROWS, n)
    row = lambda w: pl.BlockSpec((tm, w), lambda i: (i, 0))
    full = lambda a: pl.BlockSpec(a.shape, lambda i: (0, 0))
    return pl.pallas_call(
        _outproj_kernel,
        grid=(n // tm,),
        in_specs=[row(o_sb.shape[1]), row(o_fx.shape[1]), row(d), full(g_sb), full(g_fx),
                  full(w_out), full(g_post)],
        out_specs=row(d),
        out_shape=jax.ShapeDtypeStruct((n, d), F32),
        compiler_params=_params(1),
        name="out_proj",
    )(o_sb, o_fx, x2, g_sb, g_fx, w_out, g_post)


def _ple_kernel(x_ref, p_ref, gpre_ref, wgate_ref, wproj_ref, gpost_ref, o_ref):
    x = x_ref[...]
    hn = _rms(x, gpre_ref[...]).astype(MXU_DTYPE)
    gate = jax.nn.sigmoid(_dot(hn, wgate_ref[...]))
    e = gate * _dot(p_ref[...].astype(MXU_DTYPE), wproj_ref[...])
    o_ref[...] = x + _rms(e, gpost_ref[...])


def _ple(x2, p2, g_pre, w_gate, w_proj, g_post):
    n, d = x2.shape
    tm = min(PROJ_ROWS, n)
    row = lambda w: pl.BlockSpec((tm, w), lambda i: (i, 0))
    full = lambda a: pl.BlockSpec(a.shape, lambda i: (0, 0))
    return pl.pallas_call(
        _ple_kernel,
        grid=(n // tm,),
        in_specs=[row(d), row(p2.shape[1]), full(g_pre), full(w_gate), full(w_proj), full(g_post)],
        out_specs=row(d),
        out_shape=jax.ShapeDtypeStruct((n, d), F32),
        compiler_params=_params(1),
        name="ple",
    )(x2, p2, g_pre, w_gate, w_proj, g_post)


def kernel(x, p, ffn1_pre_g, ffn1_w_gate, ffn1_w_up, ffn1_w_down, ffn1_post_g, mix_pre_g, w_in, b_forget, sb_group_g, fox_group_g, w_out, mix_post_g, ffn2_pre_g, ffn2_w_gate, ffn2_w_up, ffn2_w_down, ffn2_post_g, ple_pre_g, w_ple_gate, w_ple_proj, ple_post_g):
    b, s, d = x.shape
    depth = p.shape[0]
    n = b * s
    d_qkv = 3 * (N_HEADS_SB + N_HEADS_FOX) * HEAD_DIM
    w = lambda a: a.astype(MXU_DTYPE)
    x2 = x.reshape(n, d)
    for i in range(depth):
        x2 = _ffn(x2, ffn1_pre_g[i:i + 1], w(ffn1_w_gate[i]), w(ffn1_w_up[i]), w(ffn1_w_down[i]),
                  ffn1_post_g[i:i + 1])
        wf = jnp.pad(w_in[i][:, d_qkv:], ((0, 0), (0, LANES - N_HEADS_FOX)))
        bf = jnp.pad(b_forget[i:i + 1], ((0, 0), (0, LANES - N_HEADS_FOX)))
        qt_aug, k_aug, vt, fend, kn2 = _in_proj(x2.reshape(b, s, d), mix_pre_g[i:i + 1],
                                                w(w_in[i][:, :d_qkv]), w(wf), bf)
        fend = fend.reshape(b, -1, LANES)[:, :, :N_HEADS_FOX].transpose(0, 2, 1).reshape(-1)
        kn = jnp.sqrt(jnp.max(kn2[:, :, 0, :N_HEADS_FOX], axis=1)).reshape(-1)
        o_sb = _attention(_sb_kernel, "attn_sb", qt_aug, k_aug, vt, 0, N_HEADS_SB, SB_HEADS, SB_QTILES)
        o_fx = _attention(_fox_kernel, "attn_fox", qt_aug, k_aug, vt, N_HEADS_SB, N_HEADS_FOX,
                          FOX_HEADS, FOX_QTILES, prefetch=(fend, kn))
        x2 = _out_proj(o_sb.reshape(n, -1), o_fx.reshape(n, -1), x2, sb_group_g[i:i + 1],
                       fox_group_g[i:i + 1], w(w_out[i]), mix_post_g[i:i + 1])
        x2 = _ffn(x2, ffn2_pre_g[i:i + 1], w(ffn2_w_gate[i]), w(ffn2_w_up[i]), w(ffn2_w_down[i]),
                  ffn2_post_g[i:i + 1])
        x2 = _ple(x2, p[i].reshape(n, -1), ple_pre_g[i:i + 1], w(w_ple_gate[i]), w(w_ple_proj[i]),
                  ple_post_g[i:i + 1])
    return x2.reshape(b, s, d)
```

```python
import functools

import jax
import jax.numpy as jnp
from jax import lax
from jax.experimental import pallas as pl
from jax.experimental.pallas import tpu as pltpu

HEAD_DIM = 64
N_HEADS_SB = 8
N_HEADS_FOX = 8
EPS = 1e-6
FFN_RES_WEIGHT = 0.5

LANES = 128
MXU_DTYPE = jnp.bfloat16
VMEM_LIMIT_BYTES = 56 * 1024 * 1024

FFN_ROWS = 1024
FFN_COLS = 1408
PROJ_ROWS = 512
ATTN_BLOCK = 256
KEY_CHUNK = 256
SB_HEADS, SB_QTILES = 8, 1
FOX_HEADS, FOX_QTILES = 2, 4
EXP_ZERO = -105.0
LOG2E = 1.4426950408889634
FOX_SKIP = -154.0

F32 = jnp.float32
NEG_BIG = -1e30


def _dot(a, b):
    return jnp.dot(a, b, preferred_element_type=F32)


def _rms(x, g):
    return x * lax.rsqrt(jnp.mean(x * x, axis=-1, keepdims=True) + EPS) * g


def _split3(x):
    hi = x.astype(MXU_DTYPE).astype(F32)
    r = x - hi
    mid = r.astype(MXU_DTYPE).astype(F32)
    lo = (r - mid).astype(MXU_DTYPE).astype(F32)
    return hi, mid, lo


def _params(n_axes):
    return pltpu.CompilerParams(
        dimension_semantics=("arbitrary",) * n_axes,
        vmem_limit_bytes=VMEM_LIMIT_BYTES,
    )


def _ffn_kernel(x_ref, pre_g_ref, wg_ref, wu_ref, wd_ref, post_g_ref, o_ref, hn_ref, acc_ref):
    c = pl.program_id(1)

    @pl.when(c == 0)
    def _():
        hn_ref[...] = _rms(x_ref[...], pre_g_ref[...]).astype(MXU_DTYPE)
        acc_ref[...] = jnp.zeros_like(acc_ref)

    hn = hn_ref[...]
    g = _dot(hn, wg_ref[...])
    u = _dot(hn, wu_ref[...])
    act = (g * jax.nn.sigmoid(g)) * u
    acc_ref[...] += _dot(act.astype(MXU_DTYPE), wd_ref[...])

    @pl.when(c == pl.num_programs(1) - 1)
    def _():
        o_ref[...] = x_ref[...] + FFN_RES_WEIGHT * _rms(acc_ref[...], post_g_ref[...])


def _ffn(x2, pre_g, wg, wu, wd, post_g):
    n, d = x2.shape
    dff = wg.shape[1]
    tm = min(FFN_ROWS, n)
    tf = FFN_COLS
    return pl.pallas_call(
        _ffn_kernel,
        grid=(n // tm, dff // tf),
        in_specs=[
            pl.BlockSpec((tm, d), lambda i, c: (i, 0)),
            pl.BlockSpec((1, d), lambda i, c: (0, 0)),
            pl.BlockSpec((d, tf), lambda i, c: (0, c)),
            pl.BlockSpec((d, tf), lambda i, c: (0, c)),
            pl.BlockSpec((tf, d), lambda i, c: (c, 0)),
            pl.BlockSpec((1, d), lambda i, c: (0, 0)),
        ],
        out_specs=pl.BlockSpec((tm, d), lambda i, c: (i, 0)),
        out_shape=jax.ShapeDtypeStruct((n, d), F32),
        scratch_shapes=[pltpu.VMEM((tm, d), MXU_DTYPE), pltpu.VMEM((tm, d), F32)],
        compiler_params=_params(2),
        name="ffn",
    )(x2, pre_g, wg, wu, wd, post_g)


def _log_sigmoid(x):
    return jnp.minimum(x, 0.0) - jnp.log(1.0 + jnp.exp(-jnp.abs(x)))


def _inproj_kernel(x_ref, g_ref, wqkv_ref, wf_ref, bf_ref,
                   qt_ref, k_ref, vt_ref, fend_ref, kn2_ref, carry_ref, *, tq, tk):
    tm = x_ref.shape[1]
    d_sb = N_HEADS_SB * HEAD_DIM
    d_fx = N_HEADS_FOX * HEAD_DIM
    si = pl.program_id(1)

    @pl.when(si == 0)
    def _():
        carry_ref[...] = jnp.zeros_like(carry_ref)

    hn = _rms(x_ref[0], g_ref[...]).astype(MXU_DTYPE)

    ls = _log_sigmoid(_dot(hn, wf_ref[...]) + bf_ref[...])
    r = lax.broadcasted_iota(jnp.int32, (tm, tm), 0)
    c = lax.broadcasted_iota(jnp.int32, (tm, tm), 1)
    tri = (r >= c).astype(MXU_DTYPE)
    ls_hi, ls_mid, ls_lo = _split3(ls)
    fcum = (_dot(tri, ls_hi.astype(MXU_DTYPE)) + _dot(tri, ls_mid.astype(MXU_DTYPE))
            + _dot(tri, ls_lo.astype(MXU_DTYPE))) + carry_ref[...]
    carry_ref[...] = fcum[tm - 1:tm, :]
    fcum = fcum * LOG2E
    for j in range(tm // tk):
        fend_ref[0, 0, j:j + 1, :] = fcum[(j + 1) * tk - 1:(j + 1) * tk, :]

    lane = lax.broadcasted_iota(jnp.int32, (tm, LANES), 1)
    lane1 = lax.broadcasted_iota(jnp.int32, (1, LANES), 1)
    low = lane < HEAD_DIM

    def head_halves(col0):
        pr = _dot(hn, wqkv_ref[:, col0:col0 + LANES])
        return pr, pltpu.roll(pr, HEAD_DIM, 1)

    def store_q(h, q_aug):
        qt = q_aug.T
        for j in range(tm // tq):
            qt_ref[0, h, j] = qt[:, j * tq:(j + 1) * tq].astype(MXU_DTYPE)

    scale = HEAD_DIM ** -0.5
    for grp in range(d_sb // LANES):
        for half, pr in enumerate(head_halves(grp * LANES)):
            store_q(2 * grp + half, jnp.where(low, pr * scale, 0.0))
        for half, pr in enumerate(head_halves(d_sb + grp * LANES)):
            k_ref[0, 2 * grp + half] = jnp.where(low, pr, 0.0).astype(MXU_DTYPE)

    q0 = 3 * d_sb
    kn2 = jnp.zeros((1, LANES), F32)
    for grp in range(d_fx // LANES):
        qs = head_halves(q0 + grp * LANES)
        ks = head_halves(q0 + d_fx + grp * LANES)
        for half in range(2):
            hf = 2 * grp + half
            f_hi, f_mid, f_lo = _split3(fcum[:, hf:hf + 1])
            q_bias = jnp.where(lane == HEAD_DIM, f_hi,
                     jnp.where(lane == HEAD_DIM + 1, f_mid,
                     jnp.where(lane == HEAD_DIM + 2, f_lo,
                     jnp.where(lane < HEAD_DIM + 6, 1.0, 0.0))))
            k_bias = jnp.where(lane < HEAD_DIM + 3, 1.0,
                     jnp.where(lane == HEAD_DIM + 3, -f_hi,
                     jnp.where(lane == HEAD_DIM + 4, -f_mid,
                     jnp.where(lane == HEAD_DIM + 5, -f_lo, 0.0))))
            store_q(N_HEADS_SB + hf, jnp.where(low, qs[half] * (scale * LOG2E), q_bias))
            k_used = jnp.where(low, ks[half], 0.0).astype(MXU_DTYPE).astype(F32)
            norm2 = jnp.max(jnp.sum(k_used * k_used, axis=1, keepdims=True), axis=0, keepdims=True)
            kn2 = jnp.where(lane1 == hf, norm2, kn2)
            k_ref[0, N_HEADS_SB + hf] = jnp.where(low, ks[half], k_bias).astype(MXU_DTYPE)
    kn2_ref[0, 0] = kn2

    def store_v(p0, col0, width):
        for grp in range(width // LANES):
            vt = _dot(hn, wqkv_ref[:, col0 + grp * LANES:col0 + (grp + 1) * LANES]).T
            for j in range(tm // tk):
                vt_ref[0, p0 + grp, j] = vt[:, j * tk:(j + 1) * tk].astype(MXU_DTYPE)

    store_v(0, 2 * d_sb, d_sb)
    store_v(d_sb // LANES, q0 + 2 * d_fx, d_fx)


def _in_proj(x3, g, wqkv, wf, bf):
    b, s, d = x3.shape
    tm = min(PROJ_ROWS, s)
    tq = min(ATTN_BLOCK, s)
    tk = min(KEY_CHUNK, s)
    nh = N_HEADS_SB + N_HEADS_FOX
    return pl.pallas_call(
        functools.partial(_inproj_kernel, tq=tq, tk=tk),
        grid=(b, s // tm),
        in_specs=[
            pl.BlockSpec((1, tm, d), lambda i, j: (i, j, 0)),
            pl.BlockSpec((1, d), lambda i, j: (0, 0)),
            pl.BlockSpec(wqkv.shape, lambda i, j: (0, 0)),
            pl.BlockSpec(wf.shape, lambda i, j: (0, 0)),
            pl.BlockSpec((1, LANES), lambda i, j: (0, 0)),
        ],
        out_specs=[
            pl.BlockSpec((1, nh, tm // tq, LANES, tq), lambda i, j: (i, 0, j, 0, 0)),
            pl.BlockSpec((1, nh, tm, LANES), lambda i, j: (i, 0, j, 0)),
            pl.BlockSpec((1, nh // 2, tm // tk, LANES, tk), lambda i, j: (i, 0, j, 0, 0)),
            pl.BlockSpec((1, 1, tm // tk, LANES), lambda i, j: (i, j, 0, 0)),
            pl.BlockSpec((1, 1, 1, LANES), lambda i, j: (i, j, 0, 0)),
        ],
        out_shape=[
            jax.ShapeDtypeStruct((b, nh, s // tq, LANES, tq), MXU_DTYPE),
            jax.ShapeDtypeStruct((b, nh, s, LANES), MXU_DTYPE),
            jax.ShapeDtypeStruct((b, nh // 2, s // tk, LANES, tk), MXU_DTYPE),
            jax.ShapeDtypeStruct((b, s // tm, tm // tk, LANES), F32),
            jax.ShapeDtypeStruct((b, s // tm, 1, LANES), F32),
        ],
        scratch_shapes=[pltpu.VMEM((1, LANES), F32)],
        compiler_params=_params(2),
        name="in_proj",
    )(x3, g, wqkv, wf, bf)


def _key_query_iota(tk, tq):
    return (lax.broadcasted_iota(jnp.int32, (tk, tq), 0), lax.broadcasted_iota(jnp.int32, (tk, tq), 1))


def _q_t(qt_ref, h):
    tiles = [qt_ref[0, h, j] for j in range(qt_ref.shape[2])]
    return tiles[0] if len(tiles) == 1 else jnp.concatenate(tiles, axis=1)


def _store_heads(o_ref, acc_t):
    for p in range(len(acc_t) // 2):
        pair_t = jnp.concatenate([acc_t[2 * p], acc_t[2 * p + 1]], axis=0)
        o_ref[0, :, p * LANES:(p + 1) * LANES] = pair_t.T


def _v_t(vt_ref, h, kc):
    half = (h % 2) * HEAD_DIM
    return vt_ref[0, h // 2, kc, half:half + HEAD_DIM, :]


def _sb_kernel(qt_ref, k_ref, vt_ref, o_ref):
    g, tq = qt_ref.shape[1], qt_ref.shape[2] * qt_ref.shape[4]
    tk = vt_ref.shape[4]
    n_diag = tq // tk
    qi = pl.program_id(2)
    key, query = _key_query_iota(tk, tq)
    ss = lax.broadcasted_iota(jnp.int32, (tk, 2 * tk), 0)
    jj = lax.broadcasted_iota(jnp.int32, (tk, 2 * tk), 1)
    suffix_t = (jnp.where(jj < tk, jj, jj - tk) >= ss).astype(MXU_DTYPE)

    def sweep(segments, state):
        zs, keeps, csums = [], [], []
        for kc, diag, _ in segments:
            rows = pl.ds(pl.multiple_of(kc * tk, tk), tk)
            keeps.append(None if diag is None else (key + diag * tk) < query)
            zs.append([_dot(k_ref[0, h, rows, :], qts[h]) for h in range(g)])
        for z_seg, keep in zip(zs, keeps):
            stacks = []
            for z in z_seg:
                sp = jnp.maximum(z, 0.0) + jnp.log(1.0 + jnp.exp(-jnp.abs(z)))
                if keep is not None:
                    sp = jnp.where(keep, sp, 0.0)
                hi = sp.astype(MXU_DTYPE)
                lo = (sp - hi.astype(F32)).astype(MXU_DTYPE)
                stacks.append(jnp.concatenate([hi, lo], axis=0))
            csums.append([_dot(suffix_t, st) for st in stacks])
        carry_r = [st[0] for st in state]
        ws = []
        for (kc, _, on), z_seg, keep, c_seg in zip(segments, zs, keeps, csums):
            w_seg = []
            for h in range(g):
                a = jnp.exp(z_seg[h] + carry_r[h] - c_seg[h])
                if keep is not None:
                    a = jnp.where(keep, a, 0.0)
                mass = c_seg[h][0:1, :]
                if on is not None:
                    a, mass = a * on, mass * on
                w_seg.append(a.astype(MXU_DTYPE))
                carry_r[h] = carry_r[h] - mass
            ws.append(w_seg)
        accs = [st[1] for st in state]
        for (kc, _, _), w_seg in zip(segments, ws):
            accs = [accs[h] + _dot(_v_t(vt_ref, h, kc), w_seg[h]) for h in range(g)]
        return tuple(zip(carry_r, accs))

    def any_live(state):
        r_max = functools.reduce(jnp.maximum, [st[0] for st in state])
        return (jnp.max(r_max) > EXP_ZERO).astype(jnp.int32)

    qts = [_q_t(qt_ref, h) for h in range(g)]
    n_left = qi * n_diag
    has_left = (n_left > 0).astype(F32)
    first = [(n_left + d, d, None) for d in reversed(range(n_diag))]
    first.append((jnp.maximum(n_left - 1, 0), None, has_left))
    state = sweep(first, ((jnp.zeros((1, tq), F32), jnp.zeros((HEAD_DIM, tq), F32)),) * g)

    def body(carry):
        i, _, st = carry
        st = sweep([(n_left - 2 - i, None, None)], st)
        return i + 1, any_live(st), st

    _, _, state = lax.while_loop(lambda c: (c[0] < n_left - 1) & (c[1] > 0), body,
                                 (jnp.int32(0), any_live(state), state))
    _store_heads(o_ref, [st[1] for st in state])


def _fox_kernel(fend_ref, kn_ref, qt_ref, k_ref, vt_ref, o_ref):
    g, tq = qt_ref.shape[1], qt_ref.shape[2] * qt_ref.shape[4]
    tk = vt_ref.shape[4]
    n_diag = tq // tk
    n_chunks = k_ref.shape[2] // tk
    bi, gi, qi = pl.program_id(0), pl.program_id(1), pl.program_id(2)

    def chunks(kcs, state):
        ss = []
        for kc in kcs:
            rows = pl.ds(pl.multiple_of(kc * tk, tk), tk)
            ss.append([_dot(k_ref[0, h, rows, :], qts[h]) for h in range(g)])
        ml = [(st[0], st[1]) for st in state]
        terms = []
        for s_chunk in ss:
            alphas, ps = [], []
            for h in range(g):
                m, l = ml[h]
                m_new = jnp.maximum(m, jnp.max(s_chunk[h], axis=0, keepdims=True))
                alpha = jnp.exp2(m - m_new)
                p = jnp.exp2(s_chunk[h] - m_new)
                ml[h] = (m_new, alpha * l + jnp.sum(p, axis=0, keepdims=True))
                alphas.append(alpha)
                ps.append(p.astype(MXU_DTYPE))
            terms.append((alphas, ps))
        accs = [st[2] for st in state]
        for kc, (alphas, ps) in zip(kcs, terms):
            accs = [alphas[h] * accs[h] + _dot(_v_t(vt_ref, h, kc), ps[h]) for h in range(g)]
        return tuple((ml[h][0], ml[h][1], accs[h]) for h in range(g))

    def widen(x, lo, fill):
        return x if lo == 0 else jnp.concatenate([jnp.full((x.shape[0], lo), fill, F32), x], axis=1)

    def block_softmax(h):
        qt = qts[h]
        ss = []
        for d in range(n_diag):
            rows = pl.ds(pl.multiple_of((qi * n_diag + d) * tk, tk), tk)
            key, query = _key_query_iota(tk, tq - d * tk)
            ss.append(jnp.where(key <= query, _dot(k_ref[0, h, rows, :], qt[:, d * tk:]), NEG_BIG))
        m = functools.reduce(jnp.maximum, [widen(jnp.max(s, axis=0, keepdims=True), d * tk, NEG_BIG)
                                           for d, s in enumerate(ss)])
        l = jnp.zeros((1, tq), F32)
        acc = jnp.zeros((HEAD_DIM, tq), F32)
        for d, s in enumerate(ss):
            p = jnp.exp2(s - m[:, d * tk:])
            l = l + widen(jnp.sum(p, axis=0, keepdims=True), d * tk, 0.0)
            acc = acc + widen(_dot(_v_t(vt_ref, h, qi * n_diag + d), p.astype(MXU_DTYPE)), d * tk, 0.0)
        return m, l, acc

    qts = [_q_t(qt_ref, h) for h in range(g)]
    state = tuple(block_softmax(h) for h in range(g))

    heads = [(bi * (N_HEADS_FOX) + gi * g + h) for h in range(g)]
    slack = []
    for h in range(g):
        qt = qts[h].astype(F32)
        qn = jnp.sqrt(jnp.sum(qt[0:HEAD_DIM] * qt[0:HEAD_DIM], axis=0, keepdims=True))
        f_t = jnp.sum(qt[HEAD_DIM:HEAD_DIM + 8], axis=0, keepdims=True) - 3.0
        bound = qn * (kn_ref[heads[h]] * 1.001) + f_t - state[h][0]
        slack.append(jnp.max(bound))

    def any_live(kc):
        kc = jnp.maximum(kc, 0)
        live = [slack[h] - fend_ref[heads[h] * n_chunks + kc] > FOX_SKIP for h in range(g)]
        return functools.reduce(jnp.logical_or, live).astype(jnp.int32)

    n_left = qi * n_diag

    def pair_body(carry):
        i, _, st = carry
        kc = n_left - 1 - i
        return i + 2, any_live(kc - 3), chunks([kc, kc - 1], st)

    def body(carry):
        i, _, st = carry
        kc = n_left - 1 - i
        return i + 1, any_live(kc - 1), chunks([kc], st)

    i, _, state = lax.while_loop(lambda c: (c[0] + 2 <= n_left) & (c[1] > 0), pair_body,
                                 (jnp.int32(0), any_live(n_left - 2), state))
    _, _, state = lax.while_loop(lambda c: (c[0] < n_left) & (c[1] > 0), body,
                                 (i, any_live(n_left - 1 - i), state))
    _store_heads(o_ref, [st[2] / st[1] for st in state])


def _attention(body, name, qt_aug, k_aug, vt, head0, n_heads, g, q_tiles, prefetch=()):
    b, _, nq, _, tq = qt_aug.shape
    s = k_aug.shape[2]
    nk, tk = vt.shape[2], vt.shape[4]
    g0 = head0 // g
    resident = dict(pipeline_mode=pl.Buffered(1)) if g == n_heads else {}
    return pl.pallas_call(
        body,
        grid_spec=pltpu.PrefetchScalarGridSpec(
            num_scalar_prefetch=len(prefetch),
            grid=(b, n_heads // g, nq // q_tiles),
            in_specs=[
                pl.BlockSpec((1, g, q_tiles, LANES, tq), lambda i, p, j, *_: (i, g0 + p, j, 0, 0)),
                pl.BlockSpec((1, g, s, LANES), lambda i, p, j, *_: (i, g0 + p, 0, 0), **resident),
                pl.BlockSpec((1, g // 2, nk, LANES, tk), lambda i, p, j, *_: (i, g0 + p, 0, 0, 0),
                             **resident),
            ],
            out_specs=pl.BlockSpec((1, q_tiles * tq, g * HEAD_DIM), lambda i, p, j, *_: (i, j, p)),
        ),
        out_shape=jax.ShapeDtypeStruct((b, nq * tq, n_heads * HEAD_DIM), F32),
        compiler_params=_params(3),
        name=name,
    )(*prefetch, qt_aug, k_aug, vt)


def _outproj_kernel(osb_ref, ofx_ref, x_ref, gsb_ref, gfx_ref, wout_ref, gpost_ref, o_ref):
    d_sb = osb_ref.shape[1]
    nsb = _rms(osb_ref[...], gsb_ref[...]).astype(MXU_DTYPE)
    nfx = _rms(ofx_ref[...], gfx_ref[...]).astype(MXU_DTYPE)
    m = _dot(nsb, wout_ref[0:d_sb, :]) + _dot(nfx, wout_ref[d_sb:, :])
    o_ref[...] = x_ref[...] + _rms(m, gpost_ref[...])


def _out_proj(o_sb, o_fx, x2, g_sb, g_fx, w_out, g_post):
    n, d = x2.shape
    tm = min(PROJ_ROWS, n)
    row = lambda w: pl.BlockSpec((tm, w), lambda i: (i, 0))
    full = lambda a: pl.BlockSpec(a.shape, lambda i: (0, 0))
    return pl.pallas_call(
        _outproj_kernel,
        grid=(n // tm,),
        in_specs=[row(o_sb.shape[1]), row(o_fx.shape[1]), row(d), full(g_sb), full(g_fx),
                  full(w_out), full(g_post)],
        out_specs=row(d),
        out_shape=jax.ShapeDtypeStruct((n, d), F32),
        compiler_params=_params(1),
        name="out_proj",
    )(o_sb, o_fx, x2, g_sb, g_fx, w_out, g_post)


def _ple_kernel(x_ref, p_ref, gpre_ref, wgate_ref, wproj_ref, gpost_ref, o_ref):
    x = x_ref[...]
    hn = _rms(x, gpre_ref[...]).astype(MXU_DTYPE)
    gate = jax.nn.sigmoid(_dot(hn, wgate_ref[...]))
    e = gate * _dot(p_ref[...].astype(MXU_DTYPE), wproj_ref[...])
    o_ref[...] = x + _rms(e, gpost_ref[...])


def _ple(x2, p2, g_pre, w_gate, w_proj, g_post):
    n, d = x2.shape
    tm = min(PROJ_ROWS, n)
    row = lambda w: pl.BlockSpec((tm, w), lambda i: (i, 0))
    full = lambda a: pl.BlockSpec(a.shape, lambda i: (0, 0))
    return pl.pallas_call(
        _ple_kernel,
        grid=(n // tm,),
        in_specs=[row(d), row(p2.shape[1]), full(g_pre), full(w_gate), full(w_proj), full(g_post)],
        out_specs=row(d),
        out_shape=jax.ShapeDtypeStruct((n, d), F32),
        compiler_params=_params(1),
        name="ple",
    )(x2, p2, g_pre, w_gate, w_proj, g_post)


def kernel(x, p, ffn1_pre_g, ffn1_w_gate, ffn1_w_up, ffn1_w_down, ffn1_post_g, mix_pre_g, w_in, b_forget, sb_group_g, fox_group_g, w_out, mix_post_g, ffn2_pre_g, ffn2_w_gate, ffn2_w_up, ffn2_w_down, ffn2_post_g, ple_pre_g, w_ple_gate, w_ple_proj, ple_post_g):
    b, s, d = x.shape
    depth = p.shape[0]
    n = b * s
    d_qkv = 3 * (N_HEADS_SB + N_HEADS_FOX) * HEAD_DIM
    w = lambda a: a.astype(MXU_DTYPE)
    x2 = x.reshape(n, d)
    for i in range(depth):
        x2 = _ffn(x2, ffn1_pre_g[i:i + 1], w(ffn1_w_gate[i]), w(ffn1_w_up[i]), w(ffn1_w_down[i]),
                  ffn1_post_g[i:i + 1])
        wf = jnp.pad(w_in[i][:, d_qkv:], ((0, 0), (0, LANES - N_HEADS_FOX)))
        bf = jnp.pad(b_forget[i:i + 1], ((0, 0), (0, LANES - N_HEADS_FOX)))
        qt_aug, k_aug, vt, fend, kn2 = _in_proj(x2.reshape(b, s, d), mix_pre_g[i:i + 1],
                                                w(w_in[i][:, :d_qkv]), w(wf), bf)
        fend = fend.reshape(b, -1, LANES)[:, :, :N_HEADS_FOX].transpose(0, 2, 1).reshape(-1)
        kn = jnp.sqrt(jnp.max(kn2[:, :, 0, :N_HEADS_FOX], axis=1)).reshape(-1)
        o_sb = _attention(_sb_kernel, "attn_sb", qt_aug, k_aug, vt, 0, N_HEADS_SB, SB_HEADS, SB_QTILES)
        o_fx = _attention(_fox_kernel, "attn_fox", qt_aug, k_aug, vt, N_HEADS_SB, N_HEADS_FOX,
                          FOX_HEADS, FOX_QTILES, prefetch=(fend, kn))
        x2 = _out_proj(o_sb.reshape(n, -1), o_fx.reshape(n, -1), x2, sb_group_g[i:i + 1],
                       fox_group_g[i:i + 1], w(w_out[i]), mix_post_g[i:i + 1])
        x2 = _ffn(x2, ffn2_pre_g[i:i + 1], w(ffn2_w_gate[i]), w(ffn2_w_up[i]), w(ffn2_w_down[i]),
                  ffn2_post_g[i:i + 1])
        x2 = _ple(x2, p[i].reshape(n, -1), ple_pre_g[i:i + 1], w(w_ple_gate[i]), w(w_ple_proj[i]),
                  ple_post_g[i:i + 1])
    return x2.reshape(b, s, d)
```

```python
import functools

import jax
import jax.numpy as jnp
from jax import lax
from jax.experimental import pallas as pl
from jax.experimental.pallas import tpu as pltpu

HEAD_DIM = 64
N_HEADS_SB = 8
N_HEADS_FOX = 8
EPS = 1e-6
FFN_RES_WEIGHT = 0.5

LANES = 128
MXU_DTYPE = jnp.bfloat16
VMEM_LIMIT_BYTES = 56 * 1024 * 1024

FFN_ROWS = 512
PROJ_ROWS = 512
ATTN_BLOCK = 256
KEY_CHUNK = 256
SB_HEADS, SB_QTILES = 8, 1
FOX_HEADS, FOX_QTILES = 2, 4
EXP_ZERO = -105.0
LOG2E = 1.4426950408889634
FOX_SKIP = -154.0

F32 = jnp.float32
NEG_BIG = -1e30


def _dot(a, b):
    return jnp.dot(a, b, preferred_element_type=F32)


def _rms(x, g):
    return x * lax.rsqrt(jnp.mean(x * x, axis=-1, keepdims=True) + EPS) * g


def _split3(x):
    hi = x.astype(MXU_DTYPE).astype(F32)
    r = x - hi
    mid = r.astype(MXU_DTYPE).astype(F32)
    lo = (r - mid).astype(MXU_DTYPE).astype(F32)
    return hi, mid, lo


def _params(n_axes):
    return pltpu.CompilerParams(
        dimension_semantics=("arbitrary",) * n_axes,
        vmem_limit_bytes=VMEM_LIMIT_BYTES,
    )


def _ffn_kernel(x_ref, pre_g_ref, wg_ref, wu_ref, wd_ref, post_g_ref, o_ref, hn_ref, acc_ref):
    c = pl.program_id(1)

    @pl.when(c == 0)
    def _():
        hn_ref[...] = _rms(x_ref[...], pre_g_ref[...]).astype(MXU_DTYPE)
        acc_ref[...] = jnp.zeros_like(acc_ref)

    hn = hn_ref[...]
    g = _dot(hn, wg_ref[...])
    u = _dot(hn, wu_ref[...])
    act = (g * jax.nn.sigmoid(g)) * u
    acc_ref[...] += _dot(act.astype(MXU_DTYPE), wd_ref[...])

    @pl.when(c == pl.num_programs(1) - 1)
    def _():
        o_ref[...] = x_ref[...] + FFN_RES_WEIGHT * _rms(acc_ref[...], post_g_ref[...])


def _ffn(x2, pre_g, wg, wu, wd, post_g):
    n, d = x2.shape
    dff = wg.shape[1]
    tm = min(FFN_ROWS, n)
    tf = dff
    resident = dict(pipeline_mode=pl.Buffered(1))
    return pl.pallas_call(
        _ffn_kernel,
        grid=(n // tm, dff // tf),
        in_specs=[
            pl.BlockSpec((tm, d), lambda i, c: (i, 0)),
            pl.BlockSpec((1, d), lambda i, c: (0, 0)),
            pl.BlockSpec((d, tf), lambda i, c: (0, c), **resident),
            pl.BlockSpec((d, tf), lambda i, c: (0, c), **resident),
            pl.BlockSpec((tf, d), lambda i, c: (c, 0), **resident),
            pl.BlockSpec((1, d), lambda i, c: (0, 0)),
        ],
        out_specs=pl.BlockSpec((tm, d), lambda i, c: (i, 0)),
        out_shape=jax.ShapeDtypeStruct((n, d), F32),
        scratch_shapes=[pltpu.VMEM((tm, d), MXU_DTYPE), pltpu.VMEM((tm, d), F32)],
        compiler_params=_params(2),
        name="ffn",
    )(x2, pre_g, wg, wu, wd, post_g)


def _log_sigmoid(x):
    return jnp.minimum(x, 0.0) - jnp.log(1.0 + jnp.exp(-jnp.abs(x)))


def _inproj_kernel(x_ref, g_ref, wqkv_ref, wf_ref, bf_ref,
                   qt_ref, k_ref, vt_ref, fend_ref, kn2_ref, carry_ref, *, tq, tk):
    tm = x_ref.shape[1]
    d_sb = N_HEADS_SB * HEAD_DIM
    d_fx = N_HEADS_FOX * HEAD_DIM
    si = pl.program_id(1)

    @pl.when(si == 0)
    def _():
        carry_ref[...] = jnp.zeros_like(carry_ref)

    hn = _rms(x_ref[0], g_ref[...]).astype(MXU_DTYPE)

    ls = _log_sigmoid(_dot(hn, wf_ref[...]) + bf_ref[...])
    r = lax.broadcasted_iota(jnp.int32, (tm, tm), 0)
    c = lax.broadcasted_iota(jnp.int32, (tm, tm), 1)
    tri = (r >= c).astype(MXU_DTYPE)
    ls_hi, ls_mid, ls_lo = _split3(ls)
    fcum = (_dot(tri, ls_hi.astype(MXU_DTYPE)) + _dot(tri, ls_mid.astype(MXU_DTYPE))
            + _dot(tri, ls_lo.astype(MXU_DTYPE))) + carry_ref[...]
    carry_ref[...] = fcum[tm - 1:tm, :]
    fcum = fcum * LOG2E
    for j in range(tm // tk):
        fend_ref[0, 0, j:j + 1, :] = fcum[(j + 1) * tk - 1:(j + 1) * tk, :]

    lane = lax.broadcasted_iota(jnp.int32, (tm, LANES), 1)
    lane1 = lax.broadcasted_iota(jnp.int32, (1, LANES), 1)
    low = lane < HEAD_DIM

    def head_halves(col0):
        pr = _dot(hn, wqkv_ref[:, col0:col0 + LANES])
        return pr, pltpu.roll(pr, HEAD_DIM, 1)

    def store_q(h, q_aug):
        qt = q_aug.T
        for j in range(tm // tq):
            qt_ref[0, h, j] = qt[:, j * tq:(j + 1) * tq].astype(MXU_DTYPE)

    scale = HEAD_DIM ** -0.5
    for grp in range(d_sb // LANES):
        for half, pr in enumerate(head_halves(grp * LANES)):
            store_q(2 * grp + half, jnp.where(low, pr * scale, 0.0))
        for half, pr in enumerate(head_halves(d_sb + grp * LANES)):
            k_ref[0, 2 * grp + half] = jnp.where(low, pr, 0.0).astype(MXU_DTYPE)

    q0 = 3 * d_sb
    kn2 = jnp.zeros((1, LANES), F32)
    for grp in range(d_fx // LANES):
        qs = head_halves(q0 + grp * LANES)
        ks = head_halves(q0 + d_fx + grp * LANES)
        for half in range(2):
            hf = 2 * grp + half
            f_hi, f_mid, f_lo = _split3(fcum[:, hf:hf + 1])
            q_bias = jnp.where(lane == HEAD_DIM, f_hi,
                     jnp.where(lane == HEAD_DIM + 1, f_mid,
                     jnp.where(lane == HEAD_DIM + 2, f_lo,
                     jnp.where(lane < HEAD_DIM + 6, 1.0, 0.0))))
            k_bias = jnp.where(lane < HEAD_DIM + 3, 1.0,
                     jnp.where(lane == HEAD_DIM + 3, -f_hi,
                     jnp.where(lane == HEAD_DIM + 4, -f_mid,
                     jnp.where(lane == HEAD_DIM + 5, -f_lo, 0.0))))
            store_q(N_HEADS_SB + hf, jnp.where(low, qs[half] * (scale * LOG2E), q_bias))
            k_used = jnp.where(low, ks[half], 0.0).astype(MXU_DTYPE).astype(F32)
            norm2 = jnp.max(jnp.sum(k_used * k_used, axis=1, keepdims=True), axis=0, keepdims=True)
            kn2 = jnp.where(lane1 == hf, norm2, kn2)
            k_ref[0, N_HEADS_SB + hf] = jnp.where(low, ks[half], k_bias).astype(MXU_DTYPE)
    kn2_ref[0, 0] = kn2

    def store_v(p0, col0, width):
        for grp in range(width // LANES):
            vt = _dot(hn, wqkv_ref[:, col0 + grp * LANES:col0 + (grp + 1) * LANES]).T
            for j in range(tm // tk):
                vt_ref[0, p0 + grp, j] = vt[:, j * tk:(j + 1) * tk].astype(MXU_DTYPE)

    store_v(0, 2 * d_sb, d_sb)
    store_v(d_sb // LANES, q0 + 2 * d_fx, d_fx)


def _in_proj(x3, g, wqkv, wf, bf):
    b, s, d = x3.shape
    tm = min(PROJ_ROWS, s)
    tq = min(ATTN_BLOCK, s)
    tk = min(KEY_CHUNK, s)
    nh = N_HEADS_SB + N_HEADS_FOX
    return pl.pallas_call(
        functools.partial(_inproj_kernel, tq=tq, tk=tk),
        grid=(b, s // tm),
        in_specs=[
            pl.BlockSpec((1, tm, d), lambda i, j: (i, j, 0)),
            pl.BlockSpec((1, d), lambda i, j: (0, 0)),
            pl.BlockSpec(wqkv.shape, lambda i, j: (0, 0)),
            pl.BlockSpec(wf.shape, lambda i, j: (0, 0)),
            pl.BlockSpec((1, LANES), lambda i, j: (0, 0)),
        ],
        out_specs=[
            pl.BlockSpec((1, nh, tm // tq, LANES, tq), lambda i, j: (i, 0, j, 0, 0)),
            pl.BlockSpec((1, nh, tm, LANES), lambda i, j: (i, 0, j, 0)),
            pl.BlockSpec((1, nh // 2, tm // tk, LANES, tk), lambda i, j: (i, 0, j, 0, 0)),
            pl.BlockSpec((1, 1, tm // tk, LANES), lambda i, j: (i, j, 0, 0)),
            pl.BlockSpec((1, 1, 1, LANES), lambda i, j: (i, j, 0, 0)),
        ],
        out_shape=[
            jax.ShapeDtypeStruct((b, nh, s // tq, LANES, tq), MXU_DTYPE),
            jax.ShapeDtypeStruct((b, nh, s, LANES), MXU_DTYPE),
            jax.ShapeDtypeStruct((b, nh // 2, s // tk, LANES, tk), MXU_DTYPE),
            jax.ShapeDtypeStruct((b, s // tm, tm // tk, LANES), F32),
            jax.ShapeDtypeStruct((b, s // tm, 1, LANES), F32),
        ],
        scratch_shapes=[pltpu.VMEM((1, LANES), F32)],
        compiler_params=_params(2),
        name="in_proj",
    )(x3, g, wqkv, wf, bf)


def _key_query_iota(tk, tq):
    return (lax.broadcasted_iota(jnp.int32, (tk, tq), 0), lax.broadcasted_iota(jnp.int32, (tk, tq), 1))


def _q_t(qt_ref, h):
    tiles = [qt_ref[0, h, j] for j in range(qt_ref.shape[2])]
    return tiles[0] if len(tiles) == 1 else jnp.concatenate(tiles, axis=1)


def _store_heads(o_ref, acc_t):
    for p in range(len(acc_t) // 2):
        pair_t = jnp.concatenate([acc_t[2 * p], acc_t[2 * p + 1]], axis=0)
        o_ref[0, :, p * LANES:(p + 1) * LANES] = pair_t.T


def _v_t(vt_ref, h, kc):
    half = (h % 2) * HEAD_DIM
    return vt_ref[0, h // 2, kc, half:half + HEAD_DIM, :]


def _sb_kernel(qt_ref, k_ref, vt_ref, o_ref):
    g, tq = qt_ref.shape[1], qt_ref.shape[2] * qt_ref.shape[4]
    tk = vt_ref.shape[4]
    n_diag = tq // tk
    qi = pl.program_id(2)
    key, query = _key_query_iota(tk, tq)
    ss = lax.broadcasted_iota(jnp.int32, (tk, 2 * tk), 0)
    jj = lax.broadcasted_iota(jnp.int32, (tk, 2 * tk), 1)
    suffix_t = (jnp.where(jj < tk, jj, jj - tk) >= ss).astype(MXU_DTYPE)

    def sweep(segments, state):
        zs, keeps, csums = [], [], []
        for kc, diag, _ in segments:
            rows = pl.ds(pl.multiple_of(kc * tk, tk), tk)
            keeps.append(None if diag is None else (key + diag * tk) < query)
            zs.append([_dot(k_ref[0, h, rows, :], qts[h]) for h in range(g)])
        for z_seg, keep in zip(zs, keeps):
            stacks = []
            for z in z_seg:
                sp = jnp.maximum(z, 0.0) + jnp.log(1.0 + jnp.exp(-jnp.abs(z)))
                if keep is not None:
                    sp = jnp.where(keep, sp, 0.0)
                hi = sp.astype(MXU_DTYPE)
                lo = (sp - hi.astype(F32)).astype(MXU_DTYPE)
                stacks.append(jnp.concatenate([hi, lo], axis=0))
            csums.append([_dot(suffix_t, st) for st in stacks])
        carry_r = [st[0] for st in state]
        ws = []
        for (kc, _, on), z_seg, keep, c_seg in zip(segments, zs, keeps, csums):
            w_seg = []
            for h in range(g):
                a = jnp.exp(z_seg[h] + carry_r[h] - c_seg[h])
                if keep is not None:
                    a = jnp.where(keep, a, 0.0)
                mass = c_seg[h][0:1, :]
                if on is not None:
                    a, mass = a * on, mass * on
                w_seg.append(a.astype(MXU_DTYPE))
                carry_r[h] = carry_r[h] - mass
            ws.append(w_seg)
        accs = [st[1] for st in state]
        for (kc, _, _), w_seg in zip(segments, ws):
            accs = [accs[h] + _dot(_v_t(vt_ref, h, kc), w_seg[h]) for h in range(g)]
        return tuple(zip(carry_r, accs))

    def any_live(state):
        r_max = functools.reduce(jnp.maximum, [st[0] for st in state])
        return (jnp.max(r_max) > EXP_ZERO).astype(jnp.int32)

    qts = [_q_t(qt_ref, h) for h in range(g)]
    n_left = qi * n_diag
    has_left = (n_left > 0).astype(F32)
    first = [(n_left + d, d, None) for d in reversed(range(n_diag))]
    first.append((jnp.maximum(n_left - 1, 0), None, has_left))
    state = sweep(first, ((jnp.zeros((1, tq), F32), jnp.zeros((HEAD_DIM, tq), F32)),) * g)

    def body(carry):
        i, _, st = carry
        st = sweep([(n_left - 2 - i, None, None)], st)
        return i + 1, any_live(st), st

    _, _, state = lax.while_loop(lambda c: (c[0] < n_left - 1) & (c[1] > 0), body,
                                 (jnp.int32(0), any_live(state), state))
    _store_heads(o_ref, [st[1] for st in state])


def _fox_kernel(fend_ref, kn_ref, qt_ref, k_ref, vt_ref, o_ref):
    g, tq = qt_ref.shape[1], qt_ref.shape[2] * qt_ref.shape[4]
    tk = vt_ref.shape[4]
    n_diag = tq // tk
    n_chunks = k_ref.shape[2] // tk
    bi, gi, qi = pl.program_id(0), pl.program_id(1), pl.program_id(2)

    def chunks(kcs, state):
        ss = []
        for kc in kcs:
            rows = pl.ds(pl.multiple_of(kc * tk, tk), tk)
            ss.append([_dot(k_ref[0, h, rows, :], qts[h]) for h in range(g)])
        ml = [(st[0], st[1]) for st in state]
        terms = []
        for s_chunk in ss:
            alphas, ps = [], []
            for h in range(g):
                m, l = ml[h]
                m_new = jnp.maximum(m, jnp.max(s_chunk[h], axis=0, keepdims=True))
                alpha = jnp.exp2(m - m_new)
                p = jnp.exp2(s_chunk[h] - m_new)
                ml[h] = (m_new, alpha * l + jnp.sum(p, axis=0, keepdims=True))
                alphas.append(alpha)
                ps.append(p.astype(MXU_DTYPE))
            terms.append((alphas, ps))
        accs = [st[2] for st in state]
        for kc, (alphas, ps) in zip(kcs, terms):
            accs = [alphas[h] * accs[h] + _dot(_v_t(vt_ref, h, kc), ps[h]) for h in range(g)]
        return tuple((ml[h][0], ml[h][1], accs[h]) for h in range(g))

    def widen(x, lo, fill):
        return x if lo == 0 else jnp.concatenate([jnp.full((x.shape[0], lo), fill, F32), x], axis=1)

    def block_softmax(h):
        qt = qts[h]
        ss = []
        for d in range(n_diag):
            rows = pl.ds(pl.multiple_of((qi * n_diag + d) * tk, tk), tk)
            key, query = _key_query_iota(tk, tq - d * tk)
            ss.append(jnp.where(key <= query, _dot(k_ref[0, h, rows, :], qt[:, d * tk:]), NEG_BIG))
        m = functools.reduce(jnp.maximum, [widen(jnp.max(s, axis=0, keepdims=True), d * tk, NEG_BIG)
                                           for d, s in enumerate(ss)])
        l = jnp.zeros((1, tq), F32)
        acc = jnp.zeros((HEAD_DIM, tq), F32)
        for d, s in enumerate(ss):
            p = jnp.exp2(s - m[:, d * tk:])
            l = l + widen(jnp.sum(p, axis=0, keepdims=True), d * tk, 0.0)
            acc = acc + widen(_dot(_v_t(vt_ref, h, qi * n_diag + d), p.astype(MXU_DTYPE)), d * tk, 0.0)
        return m, l, acc

    qts = [_q_t(qt_ref, h) for h in range(g)]
    state = tuple(block_softmax(h) for h in range(g))

    heads = [(bi * (N_HEADS_FOX) + gi * g + h) for h in range(g)]
    slack = []
    for h in range(g):
        qt = qts[h].astype(F32)
        qn = jnp.sqrt(jnp.sum(qt[0:HEAD_DIM] * qt[0:HEAD_DIM], axis=0, keepdims=True))
        f_t = jnp.sum(qt[HEAD_DIM:HEAD_DIM + 8], axis=0, keepdims=True) - 3.0
        bound = qn * (kn_ref[heads[h]] * 1.001) + f_t - state[h][0]
        slack.append(jnp.max(bound))

    def any_live(kc):
        kc = jnp.maximum(kc, 0)
        live = [slack[h] - fend_ref[heads[h] * n_chunks + kc] > FOX_SKIP for h in range(g)]
        return functools.reduce(jnp.logical_or, live).astype(jnp.int32)

    n_left = qi * n_diag

    def pair_body(carry):
        i, _, st = carry
        kc = n_left - 1 - i
        return i + 2, any_live(kc - 3), chunks([kc, kc - 1], st)

    def body(carry):
        i, _, st = carry
        kc = n_left - 1 - i
        return i + 1, any_live(kc - 1), chunks([kc], st)

    i, _, state = lax.while_loop(lambda c: (c[0] + 2 <= n_left) & (c[1] > 0), pair_body,
                                 (jnp.int32(0), any_live(n_left - 2), state))
    _, _, state = lax.while_loop(lambda c: (c[0] < n_left) & (c[1] > 0), body,
                                 (i, any_live(n_left - 1 - i), state))
    _store_heads(o_ref, [st[2] / st[1] for st in state])


def _attention(body, name, qt_aug, k_aug, vt, head0, n_heads, g, q_tiles, prefetch=()):
    b, _, nq, _, tq = qt_aug.shape
    s = k_aug.shape[2]
    nk, tk = vt.shape[2], vt.shape[4]
    g0 = head0 // g
    resident = dict(pipeline_mode=pl.Buffered(1)) if g == n_heads else {}
    return pl.pallas_call(
        body,
        grid_spec=pltpu.PrefetchScalarGridSpec(
            num_scalar_prefetch=len(prefetch),
            grid=(b, n_heads // g, nq // q_tiles),
            in_specs=[
                pl.BlockSpec((1, g, q_tiles, LANES, tq), lambda i, p, j, *_: (i, g0 + p, j, 0, 0)),
                pl.BlockSpec((1, g, s, LANES), lambda i, p, j, *_: (i, g0 + p, 0, 0), **resident),
                pl.BlockSpec((1, g // 2, nk, LANES, tk), lambda i, p, j, *_: (i, g0 + p, 0, 0, 0),
                             **resident),
            ],
            out_specs=pl.BlockSpec((1, q_tiles * tq, g * HEAD_DIM), lambda i, p, j, *_: (i, j, p)),
        ),
        out_shape=jax.ShapeDtypeStruct((b, nq * tq, n_heads * HEAD_DIM), F32),
        compiler_params=_params(3),
        name=name,
    )(*prefetch, qt_aug, k_aug, vt)


def _outproj_kernel(osb_ref, ofx_ref, x_ref, gsb_ref, gfx_ref, wout_ref, gpost_ref, o_ref):
    d_sb = osb_ref.shape[1]
    nsb = _rms(osb_ref[...], gsb_ref[...]).astype(MXU_DTYPE)
    nfx = _rms(ofx_ref[...], gfx_ref[...]).astype(MXU_DTYPE)
    m = _dot(nsb, wout_ref[0:d_sb, :]) + _dot(nfx, wout_ref[d_sb:, :])
    o_ref[...] = x_ref[...] + _rms(m, gpost_ref[...])


def _out_proj(o_sb, o_fx, x2, g_sb, g_fx, w_out, g_post):
    n, d = x2.shape
    tm = min(PROJ_ROWS, n)
    row = lambda w: pl.BlockSpec((tm, w), lambda i: (i, 0))
    full = lambda a: pl.BlockSpec(a.shape, lambda i: (0, 0))
    return pl.pallas_call(
        _outproj_kernel,
        grid=(n // tm,),
        in_specs=[row(o_sb.shape[1]), row(o_fx.shape[1]), row(d), full(g_sb), full(g_fx),
                  full(w_out), full(g_post)],
        out_specs=row(d),
        out_shape=jax.ShapeDtypeStruct((n, d), F32),
        compiler_params=_params(1),
        name="out_proj",
    )(o_sb, o_fx, x2, g_sb, g_fx, w_out, g_post)


def _ple_kernel(x_ref, p_ref, gpre_ref, wgate_ref, wproj_ref, gpost_ref, o_ref):
    x = x_ref[...]
    hn = _rms(x, gpre_ref[...]).astype(MXU_DTYPE)
    gate = jax.nn.sigmoid(_dot(hn, wgate_ref[...]))
    e = gate * _dot(p_ref[...].astype(MXU_DTYPE), wproj_ref[...])
    o_ref[...] = x + _rms(e, gpost_ref[...])


def _ple(x2, p2, g_pre, w_gate, w_proj, g_post):
    n, d = x2.shape
    tm = min(PROJ_ROWS, n)
    row = lambda w: pl.BlockSpec((tm, w), lambda i: (i, 0))
    full = lambda a: pl.BlockSpec(a.shape, lambda i: (0, 0))
    return pl.pallas_call(
        _ple_kernel,
        grid=(n // tm,),
        in_specs=[row(d), row(p2.shape[1]), full(g_pre), full(w_gate), full(w_proj), full(g_post)],
        out_specs=row(d),
        out_shape=jax.ShapeDtypeStruct((n, d), F32),
        compiler_params=_params(1),
        name="ple",
    )(x2, p2, g_pre, w_gate, w_proj, g_post)


def kernel(x, p, ffn1_pre_g, ffn1_w_gate, ffn1_w_up, ffn1_w_down, ffn1_post_g, mix_pre_g, w_in, b_forget, sb_group_g, fox_group_g, w_out, mix_post_g, ffn2_pre_g, ffn2_w_gate, ffn2_w_up, ffn2_w_down, ffn2_post_g, ple_pre_g, w_ple_gate, w_ple_proj, ple_post_g):
    b, s, d = x.shape
    depth = p.shape[0]
    n = b * s
    d_qkv = 3 * (N_HEADS_SB + N_HEADS_FOX) * HEAD_DIM
    w = lambda a: a.astype(MXU_DTYPE)
    x2 = x.reshape(n, d)
    for i in range(depth):
        x2 = _ffn(x2, ffn1_pre_g[i:i + 1], w(ffn1_w_gate[i]), w(ffn1_w_up[i]), w(ffn1_w_down[i]),
                  ffn1_post_g[i:i + 1])
        wf = jnp.pad(w_in[i][:, d_qkv:], ((0, 0), (0, LANES - N_HEADS_FOX)))
        bf = jnp.pad(b_forget[i:i + 1], ((0, 0), (0, LANES - N_HEADS_FOX)))
        qt_aug, k_aug, vt, fend, kn2 = _in_proj(x2.reshape(b, s, d), mix_pre_g[i:i + 1],
                                                w(w_in[i][:, :d_qkv]), w(wf), bf)
        fend = fend.reshape(b, -1, LANES)[:, :, :N_HEADS_FOX].transpose(0, 2, 1).reshape(-1)
        kn = jnp.sqrt(jnp.max(kn2[:, :, 0, :N_HEADS_FOX], axis=1)).reshape(-1)
        o_sb = _attention(_sb_kernel, "attn_sb", qt_aug, k_aug, vt, 0, N_HEADS_SB, SB_HEADS, SB_QTILES)
        o_fx = _attention(_fox_kernel, "attn_fox", qt_aug, k_aug, vt, N_HEADS_SB, N_HEADS_FOX,
                          FOX_HEADS, FOX_QTILES, prefetch=(fend, kn))
        x2 = _out_proj(o_sb.reshape(n, -1), o_fx.reshape(n, -1), x2, sb_group_g[i:i + 1],
                       fox_group_g[i:i + 1], w(w_out[i]), mix_post_g[i:i + 1])
        x2 = _ffn(x2, ffn2_pre_g[i:i + 1], w(ffn2_w_gate[i]), w(ffn2_w_up[i]), w(ffn2_w_down[i]),
                  ffn2_post_g[i:i + 1])
        x2 = _ple(x2, p[i].reshape(n, -1), ple_pre_g[i:i + 1], w(w_ple_gate[i]), w(w_ple_proj[i]),
                  ple_post_g[i:i + 1])
    return x2.reshape(b, s, d)
```
